```python
import math
import jax, jax.numpy as jnp
from jax import lax
import numpy as np

D_MODEL = 2048
BATCH = 2
SEQ = 16384
DEPTH = 4

GRID_W = 64
CTX_LEN = 256
ATT_HEADS = 8
ATT_QK_DIM = 64
ATT_V_DIM = 2 * ATT_QK_DIM
ATT_WIDTH = ATT_HEADS * ATT_V_DIM
Q_BLOCK = 128
ROPE_BASE = 10000.0
SGU_CHUNK = 128
SGU_GROUPS = 8
SGU_WIDTH = 1024
CONV_WIDTH = 1024
CONV_K = 31
IN_COLS = 3 * ATT_WIDTH + 2 * SGU_WIDTH + 2 * CONV_WIDTH
N_BRANCH = 3
FFN_DIM = 5632
N_EXPERTS = 8
TOP_K = 2
EXPERT_DIM = 2816
N_DENSE = (DEPTH + 1) // 2
N_MOE = DEPTH // 2
EPS = 1e-6

kernel_name = "hybrid_diffattn_sgu_conformer_moe_dit"


def rmsnorm(x, g):
    xf = x.astype(jnp.float32)
    y = xf * lax.rsqrt(jnp.mean(xf * xf, axis=-1, keepdims=True) + EPS)
    return y.astype(x.dtype) * g


def layernorm(x, g, b):
    xf = x.astype(jnp.float32)
    mu = jnp.mean(xf, axis=-1, keepdims=True)
    var = jnp.mean(jnp.square(xf - mu), axis=-1, keepdims=True)
    return ((xf - mu) * lax.rsqrt(var + EPS)).astype(x.dtype) * g + b


def modulate(h, shift, scale):
    return h * (1 + scale) + shift


def axial_angles(n):
    rows = n // GRID_W
    row = jnp.broadcast_to(jnp.arange(rows, dtype=jnp.int32)[:, None], (rows, GRID_W)).reshape(-1)
    col = jnp.broadcast_to(jnp.arange(GRID_W, dtype=jnp.int32)[None, :], (rows, GRID_W)).reshape(-1)
    half = ATT_QK_DIM // 2
    inv = ROPE_BASE ** (-jnp.arange(0, half, 2, dtype=jnp.float32) / half)
    ang_r = row.astype(jnp.float32)[:, None] * inv
    ang_c = col.astype(jnp.float32)[:, None] * inv
    return (jnp.cos(ang_r), jnp.sin(ang_r), jnp.cos(ang_c), jnp.sin(ang_c))


def _rot(x, cos, sin):
    x1, x2 = jnp.split(x, 2, axis=-1)
    return jnp.concatenate([x1 * cos - x2 * sin, x2 * cos + x1 * sin], axis=-1)


def rope2d(x, rope):
    cr, sr, cc, sc = [a[None, :, None, None, :].astype(x.dtype) for a in rope]
    xr, xc = jnp.split(x, 2, axis=-1)
    return jnp.concatenate([_rot(xr, cr, sr), _rot(xc, cc, sc)], axis=-1)


def heads_qk(t):
    return t.reshape(t.shape[0], t.shape[1], ATT_HEADS, 2, ATT_QK_DIM)


def heads_v(t):
    return t.reshape(t.shape[0], t.shape[1], ATT_HEADS, ATT_V_DIM)


def split_proj(p):
    return jnp.split(p, [ATT_WIDTH, 2 * ATT_WIDTH, 3 * ATT_WIDTH, 3 * ATT_WIDTH + 2 * SGU_WIDTH], axis=-1)


def diff_probs(q, k, lam):
    s = jnp.einsum('bqhcd,bkhcd->bhcqk', q, k).astype(jnp.float32) * (ATT_QK_DIM ** -0.5)
    p = jax.nn.softmax(s, axis=-1)
    return p[:, :, 0] - lam * p[:, :, 1]


def latent_attention(q, k, v, lam):
    b, s = q.shape[0], q.shape[1]
    nblk = s // Q_BLOCK
    qb = q.reshape(b, nblk, Q_BLOCK, ATT_HEADS, 2, ATT_QK_DIM).swapaxes(0, 1)

    def one_block(qi):
        a = diff_probs(qi, k, lam).astype(v.dtype)
        return jnp.einsum('bhqk,bkhd->bqhd', a, v)

    o = lax.map(one_block, qb)
    return o.swapaxes(0, 1).reshape(b, s, ATT_HEADS, ATT_V_DIM)


def diff_post(o, subln_g, lam_init):
    o = rmsnorm(o, subln_g) * (1 - lam_init)
    return o.reshape(o.shape[0], o.shape[1], ATT_WIDTH)


def spatial_gating(gm, ln_g, ln_b, w_s, b_s):
    b, n = gm.shape[0], gm.shape[1]
    u, vv = jnp.split(jax.nn.gelu(gm), 2, axis=-1)
    vv = layernorm(vv, ln_g, ln_b)
    vv = vv.reshape(b, n // SGU_CHUNK, SGU_CHUNK, SGU_GROUPS, SGU_WIDTH // SGU_GROUPS)
    mixed = jnp.einsum('gpq,bnqgc->bnpgc', w_s, vv) + b_s.T[None, None, :, :, None]
    return u * mixed.reshape(b, n, SGU_WIDTH)


def conformer_conv(cv, w_dw, b_dw, ln_g, ln_b):
    a, g = jnp.split(cv, 2, axis=-1)
    h = a * jax.nn.sigmoid(g)
    h = lax.conv_general_dilated(h, w_dw[:, None, :], window_strides=(1,),
                                 padding=[(CONV_K // 2, CONV_K // 2)],
                                 dimension_numbers=('NWC', 'WIO', 'NWC'),
                                 feature_group_count=CONV_WIDTH) + b_dw
    return jax.nn.silu(layernorm(h, ln_g, ln_b))


def merge_branches(h, att, sgu, conv, w_att_out, w_sgu_out, w_conv_out, w_gate, b_gate, w_o):
    g = jax.nn.sigmoid((h @ w_gate + b_gate).astype(jnp.float32)).astype(h.dtype)
    g_att, g_sgu, g_conv = jnp.split(g, N_BRANCH, axis=-1)
    y = g_att * (att @ w_att_out) + g_sgu * (sgu @ w_sgu_out) + g_conv * (conv @ w_conv_out)
    return y @ w_o


def swiglu(h, w1, w3, w2):
    return (jax.nn.silu(h @ w1) * (h @ w3)) @ w2


def moe_swiglu(h, w_r, b_r, w1, w3, w2):
    logits = (h @ w_r + b_r).astype(jnp.float32)
    top_v, top_i = lax.top_k(logits, TOP_K)
    wts = jax.nn.softmax(top_v, axis=-1)
    gate = jnp.sum(jax.nn.one_hot(top_i, N_EXPERTS, dtype=jnp.float32) * wts[..., None], axis=-2).astype(h.dtype)
    y = jnp.zeros_like(h)
    for e in range(N_EXPERTS):
        y = y + gate[..., e:e + 1] * swiglu(h, w1[e], w3[e], w2[e])
    return y


def setup_inputs(seed: int = 0) -> dict:
    key = jax.random.key(seed)
    ks = iter(jax.random.split(key, 64))

    def nrm(shape, scale):
        return jax.random.normal(next(ks), shape, dtype=jnp.float32) * scale

    def gain(shape):
        return 1.0 + nrm(shape, 0.02)

    D = D_MODEL
    return {
        "x": nrm((BATCH, SEQ, D), 1.0),
        "c": nrm((BATCH, D), 1.0),
        "ctx": nrm((BATCH, CTX_LEN, D), 1.0),
        "c_ctx": nrm((D,), 1.0),
        "w_mod": nrm((DEPTH, D, 6 * D), 0.01),
        "b_mod": nrm((DEPTH, 6 * D), 0.02),
        "norm1_g": gain((DEPTH, D)),
        "norm2_g": gain((DEPTH, D)),
        "w_in": nrm((DEPTH, D, IN_COLS), D ** -0.5),
        "lam_q1": nrm((DEPTH, ATT_QK_DIM), 0.1),
        "lam_k1": nrm((DEPTH, ATT_QK_DIM), 0.1),
        "lam_q2": nrm((DEPTH, ATT_QK_DIM), 0.1),
        "lam_k2": nrm((DEPTH, ATT_QK_DIM), 0.1),
        "subln_g": gain((DEPTH, ATT_V_DIM)),
        "w_att_out": nrm((DEPTH, ATT_WIDTH, D), ATT_WIDTH ** -0.5),
        "sgu_ln_g": gain((DEPTH, SGU_WIDTH)),
        "sgu_ln_b": nrm((DEPTH, SGU_WIDTH), 0.02),
        "w_spatial": nrm((DEPTH, SGU_GROUPS, SGU_CHUNK, SGU_CHUNK), SGU_CHUNK ** -0.5),
        "b_spatial": gain((DEPTH, SGU_GROUPS, SGU_CHUNK)),
        "w_sgu_out": nrm((DEPTH, SGU_WIDTH, D), SGU_WIDTH ** -0.5),
        "conv_w": nrm((DEPTH, CONV_K, CONV_WIDTH), CONV_K ** -0.5),
        "conv_b": nrm((DEPTH, CONV_WIDTH), 0.02),
        "conv_ln_g": gain((DEPTH, CONV_WIDTH)),
        "conv_ln_b": nrm((DEPTH, CONV_WIDTH), 0.02),
        "w_conv_out": nrm((DEPTH, CONV_WIDTH, D), CONV_WIDTH ** -0.5),
        "w_gate": nrm((DEPTH, D, N_BRANCH * D), D ** -0.5),
        "b_gate": nrm((DEPTH, N_BRANCH * D), 0.02),
        "w_o": nrm((DEPTH, D, D), D ** -0.5),
        "ffn_w1": nrm((N_DENSE, D, FFN_DIM), D ** -0.5),
        "ffn_w3": nrm((N_DENSE, D, FFN_DIM), D ** -0.5),
        "ffn_w2": nrm((N_DENSE, FFN_DIM, D), FFN_DIM ** -0.5),
        "router_w": nrm((N_MOE, D, N_EXPERTS), D ** -0.5),
        "router_b": nrm((N_MOE, N_EXPERTS), 0.01),
        "moe_w1": nrm((N_MOE, N_EXPERTS, D, EXPERT_DIM), D ** -0.5),
        "moe_w3": nrm((N_MOE, N_EXPERTS, D, EXPERT_DIM), D ** -0.5),
        "moe_w2": nrm((N_MOE, N_EXPERTS, EXPERT_DIM, D), EXPERT_DIM ** -0.5),
        "final_g": gain((D,)),
    }


def reference(x, c, ctx, c_ctx, w_mod, b_mod, norm1_g, norm2_g, w_in, lam_q1, lam_k1, lam_q2, lam_k2,
              subln_g, w_att_out, sgu_ln_g, sgu_ln_b, w_spatial, b_spatial, w_sgu_out, conv_w, conv_b,
              conv_ln_g, conv_ln_b, w_conv_out, w_gate, b_gate, w_o, ffn_w1, ffn_w3, ffn_w2,
              router_w, router_b, moe_w1, moe_w3, moe_w2, final_g):
    rope = axial_angles(x.shape[1])
    xc = ctx
    for i in range(DEPTH):
        last = i == DEPTH - 1
        m_lat = (jax.nn.silu(c) @ w_mod[i] + b_mod[i])[:, None, :]
        m_ctx = (jax.nn.silu(c_ctx) @ w_mod[i] + b_mod[i])[None, None, :]
        sh1, sc1, g1, sh2, sc2, g2 = jnp.split(m_lat, 6, axis=-1)
        csh1, csc1, cg1, csh2, csc2, cg2 = jnp.split(m_ctx, 6, axis=-1)

        lam_init = 0.8 - 0.6 * math.exp(-0.3 * i)
        lam = (jnp.exp(jnp.sum(lam_q1[i] * lam_k1[i]).astype(jnp.float32))
               - jnp.exp(jnp.sum(lam_q2[i] * lam_k2[i]).astype(jnp.float32)) + lam_init)

        def mix(h, att, gm, cv):
            return merge_branches(
                h, diff_post(att, subln_g[i], lam_init),
                spatial_gating(gm, sgu_ln_g[i], sgu_ln_b[i], w_spatial[i], b_spatial[i]),
                conformer_conv(cv, conv_w[i], conv_b[i], conv_ln_g[i], conv_ln_b[i]),
                w_att_out[i], w_sgu_out[i], w_conv_out[i], w_gate[i], b_gate[i], w_o[i])

        def ffn(h):
            if i % 2 == 0:
                j = i // 2
                return swiglu(h, ffn_w1[j], ffn_w3[j], ffn_w2[j])
            j = i // 2
            return moe_swiglu(h, router_w[j], router_b[j], moe_w1[j], moe_w3[j], moe_w2[j])

        h_lat = modulate(rmsnorm(x, norm1_g[i]), sh1, sc1)
        h_ctx = modulate(rmsnorm(xc, norm1_g[i]), csh1, csc1)
        if last:
            kc, vc = jnp.split(h_ctx @ w_in[i][:, ATT_WIDTH:3 * ATT_WIDTH], 2, axis=-1)
        else:
            qc, kc, vc, gmc, cvc = split_proj(h_ctx @ w_in[i])
        kc, vc = heads_qk(kc), heads_v(vc)

        ql, kl, vl, gml, cvl = split_proj(h_lat @ w_in[i])
        ql = rope2d(heads_qk(ql), rope)
        kl = rope2d(heads_qk(kl), rope)
        k_all = jnp.concatenate([kl, kc], axis=1)
        v_all = jnp.concatenate([heads_v(vl), vc], axis=1)
        att_l = latent_attention(ql, k_all, v_all, lam)
        y_lat = mix(h_lat, att_l, gml, cvl)
        x = x + g1 * y_lat

        if not last:
            att_c = jnp.einsum('bhqk,bkhd->bqhd', diff_probs(heads_qk(qc), kc, lam).astype(vc.dtype), vc)
            xc = xc + cg1 * mix(h_ctx, att_c, gmc, cvc)

        x = x + g2 * ffn(modulate(rmsnorm(x, norm2_g[i]), sh2, sc2))
        if not last:
            xc = xc + cg2 * ffn(modulate(rmsnorm(xc, norm2_g[i]), csh2, csc2))

    return rmsnorm(x, final_g)
```

```python
import functools
import math

import jax
import jax.numpy as jnp
from jax import lax
from jax.experimental import pallas as pl
from jax.experimental.pallas import tpu as pltpu

D_MODEL = 2048
BATCH = 2
SEQ = 16384
DEPTH = 4
GRID_W = 64
CTX_LEN = 256
ATT_HEADS = 8
ATT_QK_DIM = 64
ATT_V_DIM = 2 * ATT_QK_DIM
ATT_WIDTH = ATT_HEADS * ATT_V_DIM
ROPE_BASE = 10000.0
SGU_CHUNK = 128
SGU_GROUPS = 8
SGU_WIDTH = 1024
CONV_WIDTH = 1024
CONV_K = 31
IN_COLS = 3 * ATT_WIDTH + 2 * SGU_WIDTH + 2 * CONV_WIDTH
N_BRANCH = 3
FFN_DIM = 5632
N_EXPERTS = 8
TOP_K = 2
EXPERT_DIM = 2816
EPS = 1e-6

LANES = 128
HALO = 16
VMEM_LIMIT = 56 * 1024 * 1024
BF16 = jnp.bfloat16
F32 = jnp.float32


def _params(*sem):
    return pltpu.CompilerParams(dimension_semantics=sem, vmem_limit_bytes=VMEM_LIMIT)


def _tile(n, pref):
    t = min(pref, n)
    while n % t:
        t //= 2
    return t


def _normmod_kernel(x_ref, g_ref, sh_ref, sc_ref, o_ref):
    x = x_ref[0]
    y = x * lax.rsqrt(jnp.mean(x * x, axis=-1, keepdims=True) + EPS)
    y = y * g_ref[...]
    o_ref[0] = (y * (1 + sc_ref[0]) + sh_ref[0]).astype(o_ref.dtype)


def normmod(x, g, shift, scale, out_dtype):
    b, n, d = x.shape
    tn = _tile(n, 512)
    return pl.pallas_call(
        _normmod_kernel,
        grid=(b, n // tn),
        in_specs=[
            pl.BlockSpec((1, tn, d), lambda bi, i: (bi, i, 0)),
            pl.BlockSpec((1, d), lambda bi, i: (0, 0)),
            pl.BlockSpec((1, 1, d), lambda bi, i: (bi, 0, 0)),
            pl.BlockSpec((1, 1, d), lambda bi, i: (bi, 0, 0)),
        ],
        out_specs=pl.BlockSpec((1, tn, d), lambda bi, i: (bi, i, 0)),
        out_shape=jax.ShapeDtypeStruct((b, n, d), out_dtype),
        compiler_params=_params("parallel", "parallel"),
        name="normmod",
    )(x, g.reshape(1, d), shift, scale)


def _mod_kernel(c_ref, w_ref, b_ref, o_ref):
    c = c_ref[...]
    a = (c * jax.nn.sigmoid(c)).astype(BF16)
    o_ref[0] = jnp.dot(a, w_ref[0].astype(BF16), preferred_element_type=F32) + b_ref[0]


def modulation(c_rows, w_mod, b_mod):
    r, d = c_rows.shape
    nl, _, n = w_mod.shape
    tn = _tile(n, 1024)
    return pl.pallas_call(
        _mod_kernel,
        grid=(nl, n // tn),
        in_specs=[
            pl.BlockSpec((r, d), lambda l, j: (0, 0)),
            pl.BlockSpec((1, d, tn), lambda l, j: (l, 0, j)),
            pl.BlockSpec((1, 1, tn), lambda l, j: (l, 0, j)),
        ],
        out_specs=pl.BlockSpec((1, r, tn), lambda l, j: (l, 0, j)),
        out_shape=jax.ShapeDtypeStruct((nl, r, n), F32),
        compiler_params=_params("parallel", "parallel"),
        name="modulation",
    )(c_rows, w_mod, b_mod.reshape(nl, 1, n))


def _mm_kernel(a_ref, b_ref, o_ref):
    o_ref[...] = jnp.dot(a_ref[...], b_ref[...], preferred_element_type=F32).astype(o_ref.dtype)


def matmul(a, b, out_dtype):
    m, k = a.shape
    n = b.shape[1]
    tm, tn = _tile(m, 512), _tile(n, 1024)
    return pl.pallas_call(
        _mm_kernel,
        grid=(m // tm, n // tn),
        in_specs=[
            pl.BlockSpec((tm, k), lambda i, j: (i, 0)),
            pl.BlockSpec((k, tn), lambda i, j: (0, j)),
        ],
        out_specs=pl.BlockSpec((tm, tn), lambda i, j: (i, j)),
        out_shape=jax.ShapeDtypeStruct((m, n), out_dtype),
        compiler_params=_params("parallel", "parallel"),
        name="matmul",
    )(a, b)


def _swap_halves(x):
    n = x.shape[-1]
    quarter = ATT_QK_DIM // 4
    lane = lax.broadcasted_iota(jnp.int32, x.shape, x.ndim - 1)
    up = pltpu.roll(x, n - quarter, x.ndim - 1)
    down = pltpu.roll(x, quarter, x.ndim - 1)
    return jnp.where(lane % (2 * quarter) < quarter, up, down)


def _rope_kernel(q_ref, k_ref, cos_ref, sin_ref, qo_ref, ko_ref):
    cos = cos_ref[...]
    sin = sin_ref[...]
    scale = ATT_QK_DIM ** -0.5
    lane = lax.broadcasted_iota(jnp.int32, cos.shape, 1)
    first = lane < ATT_QK_DIM
    for h in range(ATT_HEADS):
        cols = slice(h * LANES, (h + 1) * LANES)
        q = q_ref[0, :, cols].astype(F32)
        k = k_ref[0, :, cols].astype(F32)
        qr = (q * cos + _swap_halves(q) * sin) * scale
        kr = k * cos + _swap_halves(k) * sin
        qo_ref[0, h, 0] = jnp.where(first, qr, 0.0).astype(qo_ref.dtype)
        qo_ref[0, h, 1] = jnp.where(first, 0.0, qr).astype(qo_ref.dtype)
        ko_ref[0, :, cols] = kr.astype(ko_ref.dtype)


def rope_qk(proj, cos, sin):
    b, n, _ = proj.shape
    tn = _tile(n, 512)
    w = ATT_WIDTH
    return pl.pallas_call(
        _rope_kernel,
        grid=(b, n // tn),
        in_specs=[
            pl.BlockSpec((1, tn, w), lambda bi, i: (bi, i, 0)),
            pl.BlockSpec((1, tn, w), lambda bi, i: (bi, i, 1)),
            pl.BlockSpec((tn, LANES), lambda bi, i: (i, 0)),
            pl.BlockSpec((tn, LANES), lambda bi, i: (i, 0)),
        ],
        out_specs=[
            pl.BlockSpec((1, ATT_HEADS, 2, tn, LANES), lambda bi, i: (bi, 0, 0, i, 0)),
            pl.BlockSpec((1, tn, w), lambda bi, i: (bi, i, 0)),
        ],
        out_shape=[
            jax.ShapeDtypeStruct((b, ATT_HEADS, 2, n, LANES), BF16),
            jax.ShapeDtypeStruct((b, n, w), BF16),
        ],
        compiler_params=_params("parallel", "parallel"),
        name="rope_qk",
    )(proj, proj, cos, sin)


def rope_tables(n):
    rows = n // GRID_W
    row = jnp.broadcast_to(jnp.arange(rows, dtype=jnp.int32)[:, None], (rows, GRID_W)).reshape(-1)
    col = jnp.broadcast_to(jnp.arange(GRID_W, dtype=jnp.int32)[None, :], (rows, GRID_W)).reshape(-1)
    half = ATT_QK_DIM // 2
    inv = ROPE_BASE ** (-jnp.arange(0, half, 2, dtype=F32) / half)
    ang_r = row.astype(F32)[:, None] * inv
    ang_c = col.astype(F32)[:, None] * inv
    cr, sr, cc, sc = jnp.cos(ang_r), jnp.sin(ang_r), jnp.cos(ang_c), jnp.sin(ang_c)
    cos = jnp.concatenate([cr, cr, cc, cc], axis=-1)
    sin = jnp.concatenate([-sr, sr, -sc, sc], axis=-1)
    reps = LANES // ATT_QK_DIM
    return jnp.tile(cos, (1, reps)), jnp.tile(sin, (1, reps))


def _attn_kernel(lam_ref, q_ref, kt_ref, v_ref, g_ref, o_ref, m_ref, l_ref, acc_ref, *, nk, tk, tq, post_scale):
    q = q_ref[0, 0].reshape(2 * tq, LANES)
    m_ref[...] = jnp.full(m_ref.shape, -jnp.inf, F32)
    l_ref[...] = jnp.zeros(l_ref.shape, F32)
    acc_ref[...] = jnp.zeros(acc_ref.shape, F32)

    def step(j, carry):
        s = jnp.dot(q, kt_ref[0, 0, j], preferred_element_type=F32)
        m_old = m_ref[...]
        m_new = jnp.maximum(m_old, jnp.max(s, axis=-1, keepdims=True))
        alpha = jnp.exp(m_old - m_new)
        p = jnp.exp(s - m_new)
        l_ref[...] = alpha * l_ref[...] + jnp.sum(p, axis=-1, keepdims=True)
        v = v_ref[0, pl.ds(pl.multiple_of(j * tk, tk), tk), :]
        acc_ref[...] = alpha * acc_ref[...] + jnp.dot(p.astype(BF16), v, preferred_element_type=F32)
        m_ref[...] = m_new
        return carry

    lax.fori_loop(0, nk, step, 0)

    o = acc_ref[...] / l_ref[...]
    o = o[:tq] - lam_ref[0, 0] * o[tq:]
    o = o * lax.rsqrt(jnp.mean(o * o, axis=-1, keepdims=True) + EPS)
    o_ref[0] = (o * g_ref[...] * post_scale).astype(o_ref.dtype)


def diff_attention(q, kt, v, lam, subln_g, post_scale, tq):
    b, h, _, s, _ = q.shape
    nk, tk = kt.shape[2], kt.shape[4]
    nkeys = v.shape[1]
    kern = functools.partial(_attn_kernel, nk=nk, tk=tk, tq=tq, post_scale=post_scale)
    return pl.pallas_call(
        kern,
        grid=(b, h, s // tq),
        in_specs=[
            pl.BlockSpec(memory_space=pltpu.SMEM),
            pl.BlockSpec((1, 1, 2, tq, LANES), lambda bi, hi, i: (bi, hi, 0, i, 0)),
            pl.BlockSpec((1, 1, nk, LANES, tk), lambda bi, hi, i: (bi, hi, 0, 0, 0)),
            pl.BlockSpec((1, nkeys, LANES), lambda bi, hi, i: (bi, 0, hi)),
            pl.BlockSpec((1, LANES), lambda bi, hi, i: (0, 0)),
        ],
        out_specs=pl.BlockSpec((1, tq, LANES), lambda bi, hi, i: (bi, i, hi)),
        out_shape=jax.ShapeDtypeStruct((b, s, h * LANES), BF16),
        scratch_shapes=[
            pltpu.VMEM((2 * tq, 1), F32),
            pltpu.VMEM((2 * tq, 1), F32),
            pltpu.VMEM((2 * tq, LANES), F32),
        ],
        compiler_params=_params("parallel", "parallel", "parallel"),
        name="diff_attention",
    )(lam.reshape(1, 1), q, kt, v, subln_g.reshape(1, LANES))


def keys_transposed(k, tk):
    b, nkeys, _ = k.shape
    k = k.reshape(b, nkeys // tk, tk, ATT_HEADS, LANES)
    return k.transpose(0, 3, 1, 4, 2)


def _sgu_kernel(u_ref, v_ref, g_ref, b_ref, ws_ref, bs_ref, o_ref):
    v = jax.nn.gelu(v_ref[0].astype(F32))
    mu = jnp.mean(v, axis=-1, keepdims=True)
    var = jnp.mean(jnp.square(v - mu), axis=-1, keepdims=True)
    v = ((v - mu) * lax.rsqrt(var + EPS) * g_ref[...] + b_ref[...]).astype(BF16)
    tn = v.shape[0]
    cw = SGU_WIDTH // SGU_GROUPS
    for c in range(tn // SGU_CHUNK):
        rows = slice(c * SGU_CHUNK, (c + 1) * SGU_CHUNK)
        for g in range(SGU_GROUPS):
            cols = slice(g * cw, (g + 1) * cw)
            mixed = jnp.dot(ws_ref[g], v[rows, cols], preferred_element_type=F32) + bs_ref[g]
            u = jax.nn.gelu(u_ref[0, rows, cols].astype(F32))
            o_ref[0, rows, cols] = (u * mixed).astype(o_ref.dtype)


def spatial_gating(proj, ln_g, ln_b, w_s, b_s):
    b, n, _ = proj.shape
    tn = _tile(n, 512)
    w = SGU_WIDTH
    u_blk = 3 * ATT_WIDTH // w
    return pl.pallas_call(
        _sgu_kernel,
        grid=(b, n // tn),
        in_specs=[
            pl.BlockSpec((1, tn, w), lambda bi, i: (bi, i, u_blk)),
            pl.BlockSpec((1, tn, w), lambda bi, i: (bi, i, u_blk + 1)),
            pl.BlockSpec((1, w), lambda bi, i: (0, 0)),
            pl.BlockSpec((1, w), lambda bi, i: (0, 0)),
            pl.BlockSpec((SGU_GROUPS, SGU_CHUNK, SGU_CHUNK), lambda bi, i: (0, 0, 0)),
            pl.BlockSpec((SGU_GROUPS, SGU_CHUNK, 1), lambda bi, i: (0, 0, 0)),
        ],
        out_specs=pl.BlockSpec((1, tn, w), lambda bi, i: (bi, i, 0)),
        out_shape=jax.ShapeDtypeStruct((b, n, w), BF16),
        compiler_params=_params("parallel", "parallel"),
        name="spatial_gating",
    )(proj, proj, ln_g.reshape(1, w), ln_b.reshape(1, w), w_s.astype(BF16),
      b_s.reshape(SGU_GROUPS, SGU_CHUNK, 1))


def _glu(a_ref, g_ref):
    return a_ref[0].astype(F32) * jax.nn.sigmoid(g_ref[0].astype(F32))


def _conv_kernel(a_ref, g_ref, ap_ref, gp_ref, an_ref, gn_ref, w_ref, b_ref, lg_ref, lb_ref, o_ref, h_ref, *, rc):
    i = pl.program_id(1)
    tn = a_ref.shape[1]
    h_ref[pl.ds(HALO, tn), :] = _glu(a_ref, g_ref)
    h_ref[pl.ds(0, HALO), :] = jnp.where(i > 0, _glu(ap_ref, gp_ref), 0.0)
    h_ref[pl.ds(HALO + tn, HALO), :] = jnp.where(i < pl.num_programs(1) - 1, _glu(an_ref, gn_ref), 0.0)
    first = HALO - CONV_K // 2

    def chunk(r, carry):
        r0 = pl.multiple_of(r * rc, rc)
        win = h_ref[pl.ds(r0, rc + 2 * HALO), :]
        acc = jnp.zeros((rc, CONV_WIDTH), F32) + b_ref[...]
        for k in range(CONV_K):
            acc = acc + w_ref[pl.ds(k, 1), :] * win[first + k:first + k + rc]
        mu = jnp.mean(acc, axis=-1, keepdims=True)
        var = jnp.mean(jnp.square(acc - mu), axis=-1, keepdims=True)
        y = (acc - mu) * lax.rsqrt(var + EPS) * lg_ref[...] + lb_ref[...]
        o_ref[0, pl.ds(r0, rc), :] = (y * jax.nn.sigmoid(y)).astype(o_ref.dtype)
        return carry

    lax.fori_loop(0, tn // rc, chunk, 0)


def conformer_conv(proj, w_dw, b_dw, ln_g, ln_b):
    b, n, _ = proj.shape
    tn = _tile(n, 512)
    w = CONV_WIDTH
    a_blk = (3 * ATT_WIDTH + 2 * SGU_WIDTH) // w
    nh = n // HALO
    per = tn // HALO

    def main(c):
        return pl.BlockSpec((1, tn, w), lambda bi, i: (bi, i, c))

    def prev(c):
        return pl.BlockSpec((1, HALO, w), lambda bi, i: (bi, jnp.maximum(i * per - 1, 0), c))

    def nxt(c):
        return pl.BlockSpec((1, HALO, w), lambda bi, i: (bi, jnp.minimum((i + 1) * per, nh - 1), c))

    vec = pl.BlockSpec((1, w), lambda bi, i: (0, 0))
    return pl.pallas_call(
        functools.partial(_conv_kernel, rc=16),
        grid=(b, n // tn),
        in_specs=[main(a_blk), main(a_blk + 1), prev(a_blk), prev(a_blk + 1), nxt(a_blk), nxt(a_blk + 1),
                  pl.BlockSpec((CONV_K, w), lambda bi, i: (0, 0)), vec, vec, vec],
        out_specs=pl.BlockSpec((1, tn, w), lambda bi, i: (bi, i, 0)),
        out_shape=jax.ShapeDtypeStruct((b, n, w), BF16),
        scratch_shapes=[pltpu.VMEM((tn + 2 * HALO, w), F32)],
        compiler_params=_params("parallel", "parallel"),
        name="conformer_conv",
    )(proj, proj, proj, proj, proj, proj, w_dw, b_dw.reshape(1, w), ln_g.reshape(1, w), ln_b.reshape(1, w))


def _merge_kernel(h_ref, att_ref, sgu_ref, conv_ref, wg0, wg1, wg2, bg0, bg1, bg2, wa, ws, wc, o_ref):
    h = h_ref[...]
    y = None
    for br_ref, wg, bg, wo in ((att_ref, wg0, bg0, wa), (sgu_ref, wg1, bg1, ws), (conv_ref, wg2, bg2, wc)):
        gate = jax.nn.sigmoid(jnp.dot(h, wg[...], preferred_element_type=F32) + bg[...])
        t = gate * jnp.dot(br_ref[...], wo[...], preferred_element_type=F32)
        y = t if y is None else y + t
    o_ref[...] = y.astype(o_ref.dtype)


def merge_branches(h, att, sgu, conv, w_gate, b_gate, w_att_out, w_sgu_out, w_conv_out):
    m, d = h.shape
    tm, tn = _tile(m, 512), _tile(d, 512)
    nj = d // tn
    bw = att.shape[1]

    def wg(br):
        return pl.BlockSpec((d, tn), lambda i, j: (0, br * nj + j))

    def bg(br):
        return pl.BlockSpec((1, tn), lambda i, j: (0, br * nj + j))

    row = pl.BlockSpec((tm, bw), lambda i, j: (i, 0))
    wout = pl.BlockSpec((bw, tn), lambda i, j: (0, j))
    b_gate = b_gate.reshape(1, N_BRANCH * d)
    return pl.pallas_call(
        _merge_kernel,
        grid=(m // tm, nj),
        in_specs=[pl.BlockSpec((tm, d), lambda i, j: (i, 0)), row, row, row,
                  wg(0), wg(1), wg(2), bg(0), bg(1), bg(2), wout, wout, wout],
        out_specs=pl.BlockSpec((tm, tn), lambda i, j: (i, j)),
        out_shape=jax.ShapeDtypeStruct((m, d), BF16),
        compiler_params=_params("parallel", "parallel"),
        name="merge_branches",
    )(h, att, sgu, conv, w_gate, w_gate, w_gate, b_gate, b_gate, b_gate, w_att_out, w_sgu_out, w_conv_out)


def _proj_res_kernel(y_ref, w_ref, x_ref, g_ref, o_ref):
    o_ref[...] = x_ref[...] + g_ref[0] * jnp.dot(y_ref[...], w_ref[...], preferred_element_type=F32)


def proj_residual(y, w, x, gate, rows_per_batch):
    m, k = y.shape
    d = w.shape[1]
    tm = _tile(rows_per_batch, 512)
    per = rows_per_batch // tm
    return pl.pallas_call(
        _proj_res_kernel,
        grid=(m // tm,),
        in_specs=[
            pl.BlockSpec((tm, k), lambda i: (i, 0)),
            pl.BlockSpec((k, d), lambda i: (0, 0)),
            pl.BlockSpec((tm, d), lambda i: (i, 0)),
            pl.BlockSpec((1, 1, d), lambda i: (i // per, 0, 0)),
        ],
        out_specs=pl.BlockSpec((tm, d), lambda i: (i, 0)),
        out_shape=jax.ShapeDtypeStruct((m, d), F32),
        compiler_params=_params("parallel"),
        name="proj_residual",
    )(y, w, x, gate)


def _ffn_kernel(h_ref, w1_ref, w3_ref, w2_ref, r_ref, x_ref, g_ref, o_ref, acc_ref):
    e, f = pl.program_id(1), pl.program_id(2)

    @pl.when((e == 0) & (f == 0))
    def _():
        acc_ref[...] = jnp.zeros(acc_ref.shape, F32)

    h = h_ref[...]
    a = jnp.dot(h, w1_ref[0], preferred_element_type=F32)
    b = jnp.dot(h, w3_ref[0], preferred_element_type=F32)
    z = (a * jax.nn.sigmoid(a) * b * r_ref[0]).astype(BF16)
    acc_ref[...] += jnp.dot(z, w2_ref[0], preferred_element_type=F32)

    @pl.when((e == pl.num_programs(1) - 1) & (f == pl.num_programs(2) - 1))
    def _():
        o_ref[...] = x_ref[...] + g_ref[0] * acc_ref[...]


def ffn_residual(h, w1, w3, w2, route, x, gate, rows_per_batch, tf):
    m, d = h.shape
    ne, _, fdim = w1.shape
    tm = _tile(rows_per_batch, 512)
    per = rows_per_batch // tm
    return pl.pallas_call(
        _ffn_kernel,
        grid=(m // tm, ne, fdim // tf),
        in_specs=[
            pl.BlockSpec((tm, d), lambda i, e, f: (i, 0)),
            pl.BlockSpec((1, d, tf), lambda i, e, f: (e, 0, f)),
            pl.BlockSpec((1, d, tf), lambda i, e, f: (e, 0, f)),
            pl.BlockSpec((1, tf, d), lambda i, e, f: (e, f, 0)),
            pl.BlockSpec((1, tm, 1), lambda i, e, f: (e, i, 0)),
            pl.BlockSpec((tm, d), lambda i, e, f: (i, 0)),
            pl.BlockSpec((1, 1, d), lambda i, e, f: (i // per, 0, 0)),
        ],
        out_specs=pl.BlockSpec((tm, d), lambda i, e, f: (i, 0)),
        out_shape=jax.ShapeDtypeStruct((m, d), F32),
        scratch_shapes=[pltpu.VMEM((tm, d), F32)],
        compiler_params=_params("parallel", "arbitrary", "arbitrary"),
        name="ffn_residual",
    )(h, w1, w3, w2, route, x, gate)


def _router_kernel(h_ref, w_ref, b_ref, o_ref):
    logits = jnp.dot(h_ref[...], w_ref[...], preferred_element_type=F32) + b_ref[...]
    ne = logits.shape[-1]
    idx = lax.broadcasted_iota(jnp.int32, logits.shape, 1)
    v1 = jnp.max(logits, axis=-1, keepdims=True)
    i1 = jnp.min(jnp.where(logits == v1, idx, ne), axis=-1, keepdims=True)
    rest = jnp.where(idx == i1, -jnp.inf, logits)
    v2 = jnp.max(rest, axis=-1, keepdims=True)
    i2 = jnp.min(jnp.where(rest == v2, idx, ne), axis=-1, keepdims=True)
    e2 = jnp.exp(v2 - v1)
    w1 = 1.0 / (1.0 + e2)
    w2 = e2 / (1.0 + e2)
    o_ref[...] = jnp.where(idx == i1, w1, 0.0) + jnp.where(idx == i2, w2, 0.0)


def router(h, w_r, b_r):
    m, d = h.shape
    ne = w_r.shape[1]
    tm = _tile(m, 512)
    return pl.pallas_call(
        _router_kernel,
        grid=(m // tm,),
        in_specs=[
            pl.BlockSpec((tm, d), lambda i: (i, 0)),
            pl.BlockSpec((d, ne), lambda i: (0, 0)),
            pl.BlockSpec((1, ne), lambda i: (0, 0)),
        ],
        out_specs=pl.BlockSpec((tm, ne), lambda i: (i, 0)),
        out_shape=jax.ShapeDtypeStruct((m, ne), F32),
        compiler_params=_params("parallel"),
        name="router",
    )(h, w_r.astype(BF16), b_r.reshape(1, ne))


def _mixer(x, n, h, proj, k_all, v_all, lam, lam_init, tq, tk, rope, p):
    b = x.shape[0]
    d = x.shape[-1]
    q, _ = rope
    att = diff_attention(q, keys_transposed(k_all, tk), v_all, lam, p["subln_g"], 1 - lam_init, tq)
    sgu = spatial_gating(proj, p["sgu_ln_g"], p["sgu_ln_b"], p["w_spatial"], p["b_spatial"])
    conv = conformer_conv(proj, p["conv_w"], p["conv_b"], p["conv_ln_g"], p["conv_ln_b"])
    m = b * n
    y = merge_branches(h.reshape(m, d), att.reshape(m, -1), sgu.reshape(m, -1), conv.reshape(m, -1),
                       p["w_gate"], p["b_gate"], p["w_att_out"], p["w_sgu_out"], p["w_conv_out"])
    return proj_residual(y, p["w_o"], x.reshape(m, d), p["g1"], n).reshape(b, n, d)


def _channel(x, n, p):
    b, _, d = x.shape
    m = b * n
    h = normmod(x, p["norm2_g"], p["sh2"], p["sc2"], BF16).reshape(m, d)
    if p["moe"]:
        route = router(h, p["router_w"], p["router_b"]).T.reshape(N_EXPERTS, m, 1)
        tf = _tile(p["w1"].shape[2], 256)
    else:
        route = jnp.ones((1, m, 1), F32)
        tf = _tile(p["w1"].shape[2], 512)
    return ffn_residual(h, p["w1"], p["w3"], p["w2"], route, x.reshape(m, d), p["g2"], n, tf).reshape(b, n, d)


def kernel(x, c, ctx, c_ctx, w_mod, b_mod, norm1_g, norm2_g, w_in, lam_q1, lam_k1, lam_q2, lam_k2, subln_g, w_att_out, sgu_ln_g, sgu_ln_b, w_spatial, b_spatial, w_sgu_out, conv_w, conv_b, conv_ln_g, conv_ln_b, w_conv_out, w_gate, b_gate, w_o, ffn_w1, ffn_w3, ffn_w2, router_w, router_b, moe_w1, moe_w3, moe_w2, final_g):
    b, s, d = x.shape
    nc = ctx.shape[1]
    depth = w_in.shape[0]
    cos, sin = rope_tables(s)
    ones, zeros = jnp.ones((nc, LANES), F32), jnp.zeros((nc, LANES), F32)

    c_rows = jnp.concatenate([c, c_ctx[None, :], jnp.zeros((8 - b - 1, d), F32)], axis=0)
    mods = modulation(c_rows, w_mod, b_mod)

    tk_lat = _tile(s + nc, 1280) if (s + nc) % 1280 == 0 else _tile(s + nc, 256)
    xc = ctx
    for i in range(depth):
        last = i == depth - 1
        lat = [t[:, None, :] for t in jnp.split(mods[i, :b], 6, axis=-1)]
        cx = [jnp.broadcast_to(t[:, None, :], (b, 1, d)) for t in jnp.split(mods[i, b:b + 1], 6, axis=-1)]
        lam_init = 0.8 - 0.6 * math.exp(-0.3 * i)
        lam = (jnp.exp(jnp.sum(lam_q1[i] * lam_k1[i]).astype(F32))
               - jnp.exp(jnp.sum(lam_q2[i] * lam_k2[i]).astype(F32)) + lam_init)
        shared = dict(
            norm2_g=norm2_g[i], subln_g=subln_g[i], sgu_ln_g=sgu_ln_g[i], sgu_ln_b=sgu_ln_b[i],
            w_spatial=w_spatial[i], b_spatial=b_spatial[i], conv_w=conv_w[i], conv_b=conv_b[i],
            conv_ln_g=conv_ln_g[i], conv_ln_b=conv_ln_b[i],
            w_gate=w_gate[i].astype(BF16), b_gate=b_gate[i], w_att_out=w_att_out[i].astype(BF16),
            w_sgu_out=w_sgu_out[i].astype(BF16), w_conv_out=w_conv_out[i].astype(BF16), w_o=w_o[i].astype(BF16),
            moe=i % 2 == 1)
        j = i // 2
        if i % 2 == 0:
            shared.update(w1=ffn_w1[j][None].astype(BF16), w3=ffn_w3[j][None].astype(BF16),
                          w2=ffn_w2[j][None].astype(BF16))
        else:
            shared.update(w1=moe_w1[j].astype(BF16), w3=moe_w3[j].astype(BF16), w2=moe_w2[j].astype(BF16),
                          router_w=router_w[j], router_b=router_b[j])
        p_lat = dict(shared, g1=lat[2], sh2=lat[3], sc2=lat[4], g2=lat[5])
        p_ctx = dict(shared, g1=cx[2], sh2=cx[3], sc2=cx[4], g2=cx[5])
        w_in_i = w_in[i].astype(BF16)

        h_lat = normmod(x, norm1_g[i], lat[0], lat[1], BF16)
        h_ctx = normmod(xc, norm1_g[i], cx[0], cx[1], BF16)
        proj_l = matmul(h_lat.reshape(b * s, d), w_in_i, BF16).reshape(b, s, IN_COLS)
        proj_c = matmul(h_ctx.reshape(b * nc, d), w_in_i, BF16).reshape(b, nc, IN_COLS)
        rope_l = rope_qk(proj_l, cos, sin)
        rope_c = rope_qk(proj_c, ones, zeros)
        k_all = jnp.concatenate([rope_l[1], rope_c[1]], axis=1)
        v_all = jnp.concatenate([proj_l[..., 2 * ATT_WIDTH:3 * ATT_WIDTH],
                                 proj_c[..., 2 * ATT_WIDTH:3 * ATT_WIDTH]], axis=1)
        x = _mixer(x, s, h_lat, proj_l, k_all, v_all, lam, lam_init, _tile(s, 512), tk_lat, rope_l, p_lat)
        if not last:
            xc = _mixer(xc, nc, h_ctx, proj_c, rope_c[1], proj_c[..., 2 * ATT_WIDTH:3 * ATT_WIDTH],
                        lam, lam_init, nc, nc, rope_c, p_ctx)

        x = _channel(x, s, p_lat)
        if not last:
            xc = _channel(xc, nc, p_ctx)

    zero = jnp.zeros((b, 1, d), F32)
    return normmod(x, final_g, zero, zero, F32)
```

```python
import functools
import math

import jax
import jax.numpy as jnp
from jax import lax
from jax.experimental import pallas as pl
from jax.experimental.pallas import tpu as pltpu

D_MODEL = 2048
BATCH = 2
SEQ = 16384
DEPTH = 4
GRID_W = 64
CTX_LEN = 256
ATT_HEADS = 8
ATT_QK_DIM = 64
ATT_V_DIM = 2 * ATT_QK_DIM
ATT_WIDTH = ATT_HEADS * ATT_V_DIM
ROPE_BASE = 10000.0
SGU_CHUNK = 128
SGU_GROUPS = 8
SGU_WIDTH = 1024
CONV_WIDTH = 1024
CONV_K = 31
IN_COLS = 3 * ATT_WIDTH + 2 * SGU_WIDTH + 2 * CONV_WIDTH
N_BRANCH = 3
FFN_DIM = 5632
N_EXPERTS = 8
TOP_K = 2
EXPERT_DIM = 2816
EPS = 1e-6

LANES = 128
ATT_KEY_TILE = 640
HALO = 16
VMEM_LIMIT = 56 * 1024 * 1024
BF16 = jnp.bfloat16
F32 = jnp.float32


def _params(*sem):
    return pltpu.CompilerParams(dimension_semantics=sem, vmem_limit_bytes=VMEM_LIMIT)


def _tile(n, pref):
    t = min(pref, n)
    while n % t:
        t //= 2
    return t


def _normmod_kernel(x_ref, g_ref, sh_ref, sc_ref, o_ref):
    x = x_ref[0]
    y = x * lax.rsqrt(jnp.mean(x * x, axis=-1, keepdims=True) + EPS)
    y = y * g_ref[...]
    o_ref[0] = (y * (1 + sc_ref[0]) + sh_ref[0]).astype(o_ref.dtype)


def normmod(x, g, shift, scale, out_dtype):
    b, n, d = x.shape
    tn = _tile(n, 512)
    return pl.pallas_call(
        _normmod_kernel,
        grid=(b, n // tn),
        in_specs=[
            pl.BlockSpec((1, tn, d), lambda bi, i: (bi, i, 0)),
            pl.BlockSpec((1, d), lambda bi, i: (0, 0)),
            pl.BlockSpec((1, 1, d), lambda bi, i: (bi, 0, 0)),
            pl.BlockSpec((1, 1, d), lambda bi, i: (bi, 0, 0)),
        ],
        out_specs=pl.BlockSpec((1, tn, d), lambda bi, i: (bi, i, 0)),
        out_shape=jax.ShapeDtypeStruct((b, n, d), out_dtype),
        compiler_params=_params("parallel", "parallel"),
        name="normmod",
    )(x, g.reshape(1, d), shift, scale)


def _mod_kernel(c_ref, w_ref, b_ref, o_ref):
    c = c_ref[...]
    a = (c * jax.nn.sigmoid(c)).astype(BF16)
    o_ref[0] = jnp.dot(a, w_ref[0].astype(BF16), preferred_element_type=F32) + b_ref[0]


def modulation(c_rows, w_mod, b_mod):
    r, d = c_rows.shape
    nl, _, n = w_mod.shape
    tn = _tile(n, 1024)
    return pl.pallas_call(
        _mod_kernel,
        grid=(nl, n // tn),
        in_specs=[
            pl.BlockSpec((r, d), lambda l, j: (0, 0)),
            pl.BlockSpec((1, d, tn), lambda l, j: (l, 0, j)),
            pl.BlockSpec((1, 1, tn), lambda l, j: (l, 0, j)),
        ],
        out_specs=pl.BlockSpec((1, r, tn), lambda l, j: (l, 0, j)),
        out_shape=jax.ShapeDtypeStruct((nl, r, n), F32),
        compiler_params=_params("parallel", "parallel"),
        name="modulation",
    )(c_rows, w_mod, b_mod.reshape(nl, 1, n))


def _mm_kernel(a_ref, b_ref, o_ref):
    o_ref[...] = jnp.dot(a_ref[...], b_ref[...], preferred_element_type=F32).astype(o_ref.dtype)


def matmul(a, b, out_dtype):
    m, k = a.shape
    n = b.shape[1]
    tm, tn = _tile(m, 512), _tile(n, 1024)
    return pl.pallas_call(
        _mm_kernel,
        grid=(m // tm, n // tn),
        in_specs=[
            pl.BlockSpec((tm, k), lambda i, j: (i, 0)),
            pl.BlockSpec((k, tn), lambda i, j: (0, j)),
        ],
        out_specs=pl.BlockSpec((tm, tn), lambda i, j: (i, j)),
        out_shape=jax.ShapeDtypeStruct((m, n), out_dtype),
        compiler_params=_params("parallel", "parallel"),
        name="matmul",
    )(a, b)


def _swap_halves(x):
    n = x.shape[-1]
    quarter = ATT_QK_DIM // 4
    lane = lax.broadcasted_iota(jnp.int32, x.shape, x.ndim - 1)
    up = pltpu.roll(x, n - quarter, x.ndim - 1)
    down = pltpu.roll(x, quarter, x.ndim - 1)
    return jnp.where(lane % (2 * quarter) < quarter, up, down)


def _rope_kernel(q_ref, k_ref, cos_ref, sin_ref, qo_ref, ko_ref):
    cos = cos_ref[...]
    sin = sin_ref[...]
    scale = ATT_QK_DIM ** -0.5 * math.log2(math.e)
    lane = lax.broadcasted_iota(jnp.int32, cos.shape, 1)
    first = lane < ATT_QK_DIM
    for h in range(ATT_HEADS):
        cols = slice(h * LANES, (h + 1) * LANES)
        q = q_ref[0, :, cols].astype(F32)
        k = k_ref[0, :, cols].astype(F32)
        qr = (q * cos + _swap_halves(q) * sin) * scale
        kr = k * cos + _swap_halves(k) * sin
        qo_ref[0, h, 0] = jnp.where(first, qr, 0.0).astype(qo_ref.dtype)
        qo_ref[0, h, 1] = jnp.where(first, 0.0, qr).astype(qo_ref.dtype)
        ko_ref[0, :, cols] = kr.astype(ko_ref.dtype)


def rope_qk(proj, cos, sin):
    b, n, _ = proj.shape
    tn = _tile(n, 512)
    w = ATT_WIDTH
    return pl.pallas_call(
        _rope_kernel,
        grid=(b, n // tn),
        in_specs=[
            pl.BlockSpec((1, tn, w), lambda bi, i: (bi, i, 0)),
            pl.BlockSpec((1, tn, w), lambda bi, i: (bi, i, 1)),
            pl.BlockSpec((tn, LANES), lambda bi, i: (i, 0)),
            pl.BlockSpec((tn, LANES), lambda bi, i: (i, 0)),
        ],
        out_specs=[
            pl.BlockSpec((1, ATT_HEADS, 2, tn, LANES), lambda bi, i: (bi, 0, 0, i, 0)),
            pl.BlockSpec((1, tn, w), lambda bi, i: (bi, i, 0)),
        ],
        out_shape=[
            jax.ShapeDtypeStruct((b, ATT_HEADS, 2, n, LANES), BF16),
            jax.ShapeDtypeStruct((b, n, w), BF16),
        ],
        compiler_params=_params("parallel", "parallel"),
        name="rope_qk",
    )(proj, proj, cos, sin)


def rope_tables(n):
    rows = n // GRID_W
    row = jnp.broadcast_to(jnp.arange(rows, dtype=jnp.int32)[:, None], (rows, GRID_W)).reshape(-1)
    col = jnp.broadcast_to(jnp.arange(GRID_W, dtype=jnp.int32)[None, :], (rows, GRID_W)).reshape(-1)
    half = ATT_QK_DIM // 2
    inv = ROPE_BASE ** (-jnp.arange(0, half, 2, dtype=F32) / half)
    ang_r = row.astype(F32)[:, None] * inv
    ang_c = col.astype(F32)[:, None] * inv
    cr, sr, cc, sc = jnp.cos(ang_r), jnp.sin(ang_r), jnp.cos(ang_c), jnp.sin(ang_c)
    cos = jnp.concatenate([cr, cr, cc, cc], axis=-1)
    sin = jnp.concatenate([-sr, sr, -sc, sc], axis=-1)
    reps = LANES // ATT_QK_DIM
    return jnp.tile(cos, (1, reps)), jnp.tile(sin, (1, reps))


def _attn_kernel(lam_ref, q_ref, kt_ref, v_ref, g_ref, o_ref, m_ref, acc_ref, alpha_ref, s_ref, p_ref,
                 *, nk, tk, tq, rb, post_scale):
    rows_all = 2 * tq

    def scores(j, slot):
        q = q_ref[0, 0].reshape(rows_all, LANES)
        s_ref[slot] = jnp.dot(q, kt_ref[0, 0, j], preferred_element_type=F32)

    def weighted_values(j, slot):
        v = v_ref[0, pl.ds(pl.multiple_of(j * tk, tk), tk), :]
        vo = jnp.concatenate([v, jnp.ones((tk, LANES), BF16)], axis=1)
        alpha = alpha_ref[slot]
        acc_ref[...] = (jnp.concatenate([alpha, alpha], axis=1) * acc_ref[...]
                        + jnp.dot(p_ref[slot], vo, preferred_element_type=F32))

    def softmax(slot):
        for r in range(rows_all // rb):
            rows = pl.ds(r * rb, rb)
            s = s_ref[slot, rows, :]
            m_old = m_ref[rows, :]
            m_new = jnp.maximum(m_old, jnp.max(s, axis=-1, keepdims=True))
            alpha_ref[slot, rows, :] = jnp.exp2(m_old - m_new)
            p_ref[slot, rows, :] = jnp.exp2(s - m_new[:, :1]).astype(BF16)
            m_ref[rows, :] = m_new

    def stage(j, slot):
        scores(jnp.minimum(j + 1, nk - 1), 1 - slot)
        weighted_values(jnp.maximum(j - 1, 0), 1 - slot)
        softmax(slot)

    def pair(jj, carry):
        stage(2 * jj, 0)
        stage(2 * jj + 1, 1)
        return carry

    m_ref[...] = jnp.full(m_ref.shape, -jnp.inf, F32)
    acc_ref[...] = jnp.zeros(acc_ref.shape, F32)
    p_ref[1] = jnp.zeros(p_ref.shape[1:], BF16)
    alpha_ref[1] = jnp.ones(alpha_ref.shape[1:], F32)
    scores(0, 0)
    lax.fori_loop(0, nk // 2, pair, 0)
    if nk % 2:
        stage(nk - 1, 0)
    weighted_values(nk - 1, (nk - 1) % 2)

    acc = acc_ref[...]
    o = acc[:, :LANES] / acc[:, LANES:]
    o = o[:tq] - lam_ref[0, 0] * o[tq:]
    o = o * lax.rsqrt(jnp.mean(o * o, axis=-1, keepdims=True) + EPS)
    o_ref[0] = (o * g_ref[...] * post_scale).astype(o_ref.dtype)


def diff_attention(q, kt, v, lam, subln_g, post_scale, tq):
    b, h, _, s, _ = q.shape
    nk, tk = kt.shape[2], kt.shape[4]
    nkeys = v.shape[1]
    rows_all = 2 * tq
    kern = functools.partial(_attn_kernel, nk=nk, tk=tk, tq=tq, rb=_tile(rows_all, 64), post_scale=post_scale)
    return pl.pallas_call(
        kern,
        grid=(b, h, s // tq),
        in_specs=[
            pl.BlockSpec(memory_space=pltpu.SMEM),
            pl.BlockSpec((1, 1, 2, tq, LANES), lambda bi, hi, i: (bi, hi, 0, i, 0)),
            pl.BlockSpec((1, 1, nk, LANES, tk), lambda bi, hi, i: (bi, hi, 0, 0, 0)),
            pl.BlockSpec((1, nkeys, LANES), lambda bi, hi, i: (bi, 0, hi)),
            pl.BlockSpec((1, LANES), lambda bi, hi, i: (0, 0)),
        ],
        out_specs=pl.BlockSpec((1, tq, LANES), lambda bi, hi, i: (bi, i, hi)),
        out_shape=jax.ShapeDtypeStruct((b, s, h * LANES), BF16),
        scratch_shapes=[
            pltpu.VMEM((rows_all, LANES), F32),
            pltpu.VMEM((rows_all, 2 * LANES), F32),
            pltpu.VMEM((2, rows_all, LANES), F32),
            pltpu.VMEM((2, rows_all, tk), F32),
            pltpu.VMEM((2, rows_all, tk), BF16),
        ],
        compiler_params=_params("parallel", "parallel", "parallel"),
        name="diff_attention",
    )(lam.reshape(1, 1), q, kt, v, subln_g.reshape(1, LANES))


def keys_transposed(k, tk):
    b, nkeys, _ = k.shape
    k = k.reshape(b, nkeys // tk, tk, ATT_HEADS, LANES)
    return k.transpose(0, 3, 1, 4, 2)


def _sgu_kernel(u_ref, v_ref, g_ref, b_ref, ws_ref, bs_ref, o_ref):
    v = jax.nn.gelu(v_ref[0].astype(F32))
    mu = jnp.mean(v, axis=-1, keepdims=True)
    var = jnp.mean(jnp.square(v - mu), axis=-1, keepdims=True)
    v = ((v - mu) * lax.rsqrt(var + EPS) * g_ref[...] + b_ref[...]).astype(BF16)
    tn = v.shape[0]
    cw = SGU_WIDTH // SGU_GROUPS
    for c in range(tn // SGU_CHUNK):
        rows = slice(c * SGU_CHUNK, (c + 1) * SGU_CHUNK)
        for g in range(SGU_GROUPS):
            cols = slice(g * cw, (g + 1) * cw)
            mixed = jnp.dot(ws_ref[g], v[rows, cols], preferred_element_type=F32) + bs_ref[g]
            u = jax.nn.gelu(u_ref[0, rows, cols].astype(F32))
            o_ref[0, rows, cols] = (u * mixed).astype(o_ref.dtype)


def spatial_gating(proj, ln_g, ln_b, w_s, b_s):
    b, n, _ = proj.shape
    tn = _tile(n, 512)
    w = SGU_WIDTH
    u_blk = 3 * ATT_WIDTH // w
    return pl.pallas_call(
        _sgu_kernel,
        grid=(b, n // tn),
        in_specs=[
            pl.BlockSpec((1, tn, w), lambda bi, i: (bi, i, u_blk)),
            pl.BlockSpec((1, tn, w), lambda bi, i: (bi, i, u_blk + 1)),
            pl.BlockSpec((1, w), lambda bi, i: (0, 0)),
            pl.BlockSpec((1, w), lambda bi, i: (0, 0)),
            pl.BlockSpec((SGU_GROUPS, SGU_CHUNK, SGU_CHUNK), lambda bi, i: (0, 0, 0)),
            pl.BlockSpec((SGU_GROUPS, SGU_CHUNK, 1), lambda bi, i: (0, 0, 0)),
        ],
        out_specs=pl.BlockSpec((1, tn, w), lambda bi, i: (bi, i, 0)),
        out_shape=jax.ShapeDtypeStruct((b, n, w), BF16),
        compiler_params=_params("parallel", "parallel"),
        name="spatial_gating",
    )(proj, proj, ln_g.reshape(1, w), ln_b.reshape(1, w), w_s.astype(BF16),
      b_s.reshape(SGU_GROUPS, SGU_CHUNK, 1))


def _glu(a_ref, g_ref):
    return a_ref[0].astype(F32) * jax.nn.sigmoid(g_ref[0].astype(F32))


def _conv_kernel(a_ref, g_ref, ap_ref, gp_ref, an_ref, gn_ref, w_ref, b_ref, lg_ref, lb_ref, o_ref, h_ref, *, rc):
    i = pl.program_id(1)
    tn = a_ref.shape[1]
    h_ref[pl.ds(HALO, tn), :] = _glu(a_ref, g_ref)
    h_ref[pl.ds(0, HALO), :] = jnp.where(i > 0, _glu(ap_ref, gp_ref), 0.0)
    h_ref[pl.ds(HALO + tn, HALO), :] = jnp.where(i < pl.num_programs(1) - 1, _glu(an_ref, gn_ref), 0.0)
    first = HALO - CONV_K // 2

    def chunk(r, carry):
        r0 = pl.multiple_of(r * rc, rc)
        win = h_ref[pl.ds(r0, rc + 2 * HALO), :]
        acc = jnp.zeros((rc, CONV_WIDTH), F32) + b_ref[...]
        for k in range(CONV_K):
            acc = acc + w_ref[pl.ds(k, 1), :] * win[first + k:first + k + rc]
        mu = jnp.mean(acc, axis=-1, keepdims=True)
        var = jnp.mean(jnp.square(acc - mu), axis=-1, keepdims=True)
        y = (acc - mu) * lax.rsqrt(var + EPS) * lg_ref[...] + lb_ref[...]
        o_ref[0, pl.ds(r0, rc), :] = (y * jax.nn.sigmoid(y)).astype(o_ref.dtype)
        return carry

    lax.fori_loop(0, tn // rc, chunk, 0)


def conformer_conv(proj, w_dw, b_dw, ln_g, ln_b):
    b, n, _ = proj.shape
    tn = _tile(n, 512)
    w = CONV_WIDTH
    a_blk = (3 * ATT_WIDTH + 2 * SGU_WIDTH) // w
    nh = n // HALO
    per = tn // HALO

    def main(c):
        return pl.BlockSpec((1, tn, w), lambda bi, i: (bi, i, c))

    def prev(c):
        return pl.BlockSpec((1, HALO, w), lambda bi, i: (bi, jnp.maximum(i * per - 1, 0), c))

    def nxt(c):
        return pl.BlockSpec((1, HALO, w), lambda bi, i: (bi, jnp.minimum((i + 1) * per, nh - 1), c))

    vec = pl.BlockSpec((1, w), lambda bi, i: (0, 0))
    return pl.pallas_call(
        functools.partial(_conv_kernel, rc=16),
        grid=(b, n // tn),
        in_specs=[main(a_blk), main(a_blk + 1), prev(a_blk), prev(a_blk + 1), nxt(a_blk), nxt(a_blk + 1),
                  pl.BlockSpec((CONV_K, w), lambda bi, i: (0, 0)), vec, vec, vec],
        out_specs=pl.BlockSpec((1, tn, w), lambda bi, i: (bi, i, 0)),
        out_shape=jax.ShapeDtypeStruct((b, n, w), BF16),
        scratch_shapes=[pltpu.VMEM((tn + 2 * HALO, w), F32)],
        compiler_params=_params("parallel", "parallel"),
        name="conformer_conv",
    )(proj, proj, proj, proj, proj, proj, w_dw, b_dw.reshape(1, w), ln_g.reshape(1, w), ln_b.reshape(1, w))


def _merge_kernel(h_ref, att_ref, sgu_ref, conv_ref, wg0, wg1, wg2, bg0, bg1, bg2, wa, ws, wc, o_ref):
    h = h_ref[...]
    y = None
    for br_ref, wg, bg, wo in ((att_ref, wg0, bg0, wa), (sgu_ref, wg1, bg1, ws), (conv_ref, wg2, bg2, wc)):
        gate = jax.nn.sigmoid(jnp.dot(h, wg[...], preferred_element_type=F32) + bg[...])
        t = gate * jnp.dot(br_ref[...], wo[...], preferred_element_type=F32)
        y = t if y is None else y + t
    o_ref[...] = y.astype(o_ref.dtype)


def merge_branches(h, att, sgu, conv, w_gate, b_gate, w_att_out, w_sgu_out, w_conv_out):
    m, d = h.shape
    tm, tn = _tile(m, 512), _tile(d, 512)
    nj = d // tn
    bw = att.shape[1]

    def wg(br):
        return pl.BlockSpec((d, tn), lambda i, j: (0, br * nj + j))

    def bg(br):
        return pl.BlockSpec((1, tn), lambda i, j: (0, br * nj + j))

    row = pl.BlockSpec((tm, bw), lambda i, j: (i, 0))
    wout = pl.BlockSpec((bw, tn), lambda i, j: (0, j))
    b_gate = b_gate.reshape(1, N_BRANCH * d)
    return pl.pallas_call(
        _merge_kernel,
        grid=(m // tm, nj),
        in_specs=[pl.BlockSpec((tm, d), lambda i, j: (i, 0)), row, row, row,
                  wg(0), wg(1), wg(2), bg(0), bg(1), bg(2), wout, wout, wout],
        out_specs=pl.BlockSpec((tm, tn), lambda i, j: (i, j)),
        out_shape=jax.ShapeDtypeStruct((m, d), BF16),
        compiler_params=_params("parallel", "parallel"),
        name="merge_branches",
    )(h, att, sgu, conv, w_gate, w_gate, w_gate, b_gate, b_gate, b_gate, w_att_out, w_sgu_out, w_conv_out)


def _proj_res_kernel(y_ref, w_ref, x_ref, g_ref, o_ref):
    o_ref[...] = x_ref[...] + g_ref[0] * jnp.dot(y_ref[...], w_ref[...], preferred_element_type=F32)


def proj_residual(y, w, x, gate, rows_per_batch):
    m, k = y.shape
    d = w.shape[1]
    tm = _tile(rows_per_batch, 512)
    per = rows_per_batch // tm
    return pl.pallas_call(
        _proj_res_kernel,
        grid=(m // tm,),
        in_specs=[
            pl.BlockSpec((tm, k), lambda i: (i, 0)),
            pl.BlockSpec((k, d), lambda i: (0, 0)),
            pl.BlockSpec((tm, d), lambda i: (i, 0)),
            pl.BlockSpec((1, 1, d), lambda i: (i // per, 0, 0)),
        ],
        out_specs=pl.BlockSpec((tm, d), lambda i: (i, 0)),
        out_shape=jax.ShapeDtypeStruct((m, d), F32),
        compiler_params=_params("parallel"),
        name="proj_residual",
    )(y, w, x, gate)


def _swiglu_hidden(h, w1_ref, w3_ref):
    a = jnp.dot(h, w1_ref[0], preferred_element_type=F32)
    b = jnp.dot(h, w3_ref[0], preferred_element_type=F32)
    return (a * jax.nn.sigmoid(a) * b).astype(BF16)


def _ffn_kernel(h_ref, w1_ref, w3_ref, w2_ref, x_ref, g_ref, o_ref, acc_ref):
    f = pl.program_id(1)
    y = jnp.dot(_swiglu_hidden(h_ref[...], w1_ref, w3_ref), w2_ref[0], preferred_element_type=F32)

    @pl.when(f == 0)
    def _():
        acc_ref[...] = y

    @pl.when(f > 0)
    def _():
        acc_ref[...] += y

    @pl.when(f == pl.num_programs(1) - 1)
    def _():
        o_ref[...] = x_ref[...] + g_ref[0] * acc_ref[...]


def ffn_residual(h, w1, w3, w2, x, gate, rows_per_batch):
    m, d = h.shape
    fdim = w1.shape[2]
    tm = _tile(rows_per_batch, 512)
    tf = _tile(fdim, 512)
    per = rows_per_batch // tm
    return pl.pallas_call(
        _ffn_kernel,
        grid=(m // tm, fdim // tf),
        in_specs=[
            pl.BlockSpec((tm, d), lambda i, f: (i, 0)),
            pl.BlockSpec((1, d, tf), lambda i, f: (0, 0, f)),
            pl.BlockSpec((1, d, tf), lambda i, f: (0, 0, f)),
            pl.BlockSpec((1, tf, d), lambda i, f: (0, f, 0)),
            pl.BlockSpec((tm, d), lambda i, f: (i, 0)),
            pl.BlockSpec((1, 1, d), lambda i, f: (i // per, 0, 0)),
        ],
        out_specs=pl.BlockSpec((tm, d), lambda i, f: (i, 0)),
        out_shape=jax.ShapeDtypeStruct((m, d), F32),
        scratch_shapes=[pltpu.VMEM((tm, d), F32)],
        compiler_params=_params("parallel", "arbitrary"),
        name="ffn_residual",
    )(h, w1, w3, w2, x, gate)


def _router_kernel(h_ref, w_ref, b_ref, sel_ref, wts_ref):
    logits = jnp.dot(h_ref[...].astype(BF16), w_ref[...], preferred_element_type=F32) + b_ref[...]
    ne = logits.shape[-1]
    idx = lax.broadcasted_iota(jnp.int32, logits.shape, 1)
    v1 = jnp.max(logits, axis=-1, keepdims=True)
    i1 = jnp.min(jnp.where(logits == v1, idx, ne), axis=-1, keepdims=True)
    rest = jnp.where(idx == i1, -jnp.inf, logits)
    v2 = jnp.max(rest, axis=-1, keepdims=True)
    i2 = jnp.min(jnp.where(rest == v2, idx, ne), axis=-1, keepdims=True)
    e2 = jnp.exp(v2 - v1)
    sel_ref[...] = jnp.concatenate([i1, i2], axis=1)
    wts_ref[...] = jnp.concatenate([1.0 / (1.0 + e2), e2 / (1.0 + e2)], axis=1)


def router(h, w_r, b_r):
    m, d = h.shape
    ne = w_r.shape[1]
    tm = _tile(m, 512)
    return pl.pallas_call(
        _router_kernel,
        grid=(m // tm,),
        in_specs=[
            pl.BlockSpec((tm, d), lambda i: (i, 0)),
            pl.BlockSpec((d, ne), lambda i: (0, 0)),
            pl.BlockSpec((1, ne), lambda i: (0, 0)),
        ],
        out_specs=[pl.BlockSpec((tm, TOP_K), lambda i: (i, 0)), pl.BlockSpec((tm, TOP_K), lambda i: (i, 0))],
        out_shape=[jax.ShapeDtypeStruct((m, TOP_K), jnp.int32), jax.ShapeDtypeStruct((m, TOP_K), F32)],
        compiler_params=_params("parallel"),
        name="router",
    )(h, w_r.astype(BF16), b_r.reshape(1, ne))


def dispatch_plan(sel, wts, tm):
    m = sel.shape[0]
    ne = N_EXPERTS
    npairs = m * TOP_K
    rows = npairs + ne * tm
    e_flat = sel.reshape(npairs)
    onehot = (e_flat[:, None] == jnp.arange(ne, dtype=jnp.int32)[None, :]).astype(jnp.int32)
    rank = jnp.cumsum(onehot, axis=0) - onehot
    counts = jnp.sum(onehot, axis=0)
    padded = (counts + tm - 1) // tm * tm
    ends = jnp.cumsum(padded)
    pos = (ends - padded)[e_flat] + jnp.sum(rank * onehot, axis=1)
    src = jnp.zeros((rows,), jnp.int32).at[pos].set(jnp.arange(npairs, dtype=jnp.int32) // TOP_K)
    row_w = jnp.zeros((rows,), F32).at[pos].set(wts.reshape(npairs))
    tile_start = jnp.arange(rows // tm, dtype=jnp.int32) * tm
    tile_expert = jnp.minimum(jnp.searchsorted(ends, tile_start, side="right"), ne - 1).astype(jnp.int32)
    meta = jnp.concatenate([tile_expert, (ends[-1:] // tm).astype(jnp.int32)])
    return src, row_w, pos.reshape(m, TOP_K), meta


def _gather_rows_kernel(idx_ref, src_ref, o_ref, buf_ref, sem):
    tg = buf_ref.shape[0]

    def row_copy(r, src_row):
        return pltpu.make_async_copy(src_ref.at[pl.ds(src_row, 1)], buf_ref.at[pl.ds(r, 1)], sem)

    def issue(r, carry):
        row_copy(r, idx_ref[0, 0, r]).start()
        return carry

    def drain(r, carry):
        row_copy(r, 0).wait()
        return carry

    lax.fori_loop(0, tg, issue, 0)
    lax.fori_loop(0, tg, drain, 0)
    o_ref[...] = buf_ref[...].astype(o_ref.dtype)


def gather_rows(x, idx, tg, out_dtype):
    d = x.shape[1]
    r = idx.shape[0]
    return pl.pallas_call(
        _gather_rows_kernel,
        grid=(r // tg,),
        in_specs=[
            pl.BlockSpec((1, 1, tg), lambda i: (i, 0, 0), memory_space=pltpu.SMEM),
            pl.BlockSpec(memory_space=pl.ANY),
        ],
        out_specs=pl.BlockSpec((tg, d), lambda i: (i, 0)),
        out_shape=jax.ShapeDtypeStruct((r, d), out_dtype),
        scratch_shapes=[pltpu.VMEM((tg, d), x.dtype), pltpu.SemaphoreType.DMA(())],
        compiler_params=_params("arbitrary"),
        name="gather_rows",
    )(idx.reshape(r // tg, 1, tg), x)


def _expert_ffn_kernel(meta_ref, h_ref, w1_ref, w3_ref, w2_ref, r_ref, o_ref, acc_ref):
    i, f = pl.program_id(0), pl.program_id(1)
    used = i < meta_ref[meta_ref.shape[0] - 1]

    @pl.when(used)
    def _():
        y = jnp.dot(_swiglu_hidden(h_ref[...], w1_ref, w3_ref), w2_ref[0], preferred_element_type=F32)

        @pl.when(f == 0)
        def _():
            acc_ref[...] = y

        @pl.when(f > 0)
        def _():
            acc_ref[...] += y

    @pl.when(f == pl.num_programs(1) - 1)
    def _():
        @pl.when(used)
        def _():
            o_ref[...] = r_ref[...] * acc_ref[...]

        @pl.when(jnp.logical_not(used))
        def _():
            o_ref[...] = jnp.zeros(o_ref.shape, o_ref.dtype)


def expert_ffn(hs, w1, w3, w2, row_w, meta, tm):
    r, d = hs.shape
    fdim = w1.shape[2]
    tf = _tile(fdim, 256)
    grid_spec = pltpu.PrefetchScalarGridSpec(
        num_scalar_prefetch=1,
        grid=(r // tm, fdim // tf),
        in_specs=[
            pl.BlockSpec((tm, d), lambda i, f, meta: (i, 0)),
            pl.BlockSpec((1, d, tf), lambda i, f, meta: (meta[i], 0, f)),
            pl.BlockSpec((1, d, tf), lambda i, f, meta: (meta[i], 0, f)),
            pl.BlockSpec((1, tf, d), lambda i, f, meta: (meta[i], f, 0)),
            pl.BlockSpec((tm, 1), lambda i, f, meta: (i, 0)),
        ],
        out_specs=pl.BlockSpec((tm, d), lambda i, f, meta: (i, 0)),
        scratch_shapes=[pltpu.VMEM((tm, d), F32)],
    )
    return pl.pallas_call(
        _expert_ffn_kernel,
        grid_spec=grid_spec,
        out_shape=jax.ShapeDtypeStruct((r, d), F32),
        compiler_params=_params("parallel", "arbitrary"),
        name="expert_ffn",
    )(meta, hs, w1, w3, w2, row_w.reshape(r, 1))


def _combine_kernel(pos_ref, ys_ref, x_ref, g_ref, o_ref, buf_ref, sem):
    tm = x_ref.shape[0]

    def row_copy(k, r, src_row):
        return pltpu.make_async_copy(ys_ref.at[pl.ds(src_row, 1)], buf_ref.at[k, pl.ds(r, 1)], sem)

    def issue(r, carry):
        for k in range(TOP_K):
            row_copy(k, r, pos_ref[0, k, r]).start()
        return carry

    def drain(r, carry):
        for k in range(TOP_K):
            row_copy(k, r, 0).wait()
        return carry

    lax.fori_loop(0, tm, issue, 0)
    lax.fori_loop(0, tm, drain, 0)
    y = buf_ref[0]
    for k in range(1, TOP_K):
        y = y + buf_ref[k]
    o_ref[...] = x_ref[...] + g_ref[0] * y


def combine_residual(ys, pos, x, gate, rows_per_batch):
    m, d = x.shape
    tm = _tile(rows_per_batch, 256)
    per = rows_per_batch // tm
    pos_t = pos.reshape(m // tm, tm, TOP_K).transpose(0, 2, 1)
    return pl.pallas_call(
        _combine_kernel,
        grid=(m // tm,),
        in_specs=[
            pl.BlockSpec((1, TOP_K, tm), lambda i: (i, 0, 0), memory_space=pltpu.SMEM),
            pl.BlockSpec(memory_space=pl.ANY),
            pl.BlockSpec((tm, d), lambda i: (i, 0)),
            pl.BlockSpec((1, 1, d), lambda i: (i // per, 0, 0)),
        ],
        out_specs=pl.BlockSpec((tm, d), lambda i: (i, 0)),
        out_shape=jax.ShapeDtypeStruct((m, d), F32),
        scratch_shapes=[pltpu.VMEM((TOP_K, tm, d), F32), pltpu.SemaphoreType.DMA(())],
        compiler_params=_params("arbitrary"),
        name="combine_residual",
    )(pos_t, ys, x, gate)


def moe_residual(h, w_r, b_r, w1, w3, w2, x, gate, rows_per_batch):
    m = h.shape[0]
    tm = 512 if m * TOP_K >= 8 * 512 else 128
    sel, wts = router(h, w_r, b_r)
    src, row_w, pos, meta = dispatch_plan(sel, wts, tm)
    hs = gather_rows(h, src, tm, BF16)
    ys = expert_ffn(hs, w1, w3, w2, row_w, meta, tm)
    return combine_residual(ys, pos, x, gate, rows_per_batch)


def _mixer(x, n, h, proj, k_all, v_all, lam, lam_init, tq, tk, rope, p):
    b = x.shape[0]
    d = x.shape[-1]
    q, _ = rope
    att = diff_attention(q, keys_transposed(k_all, tk), v_all, lam, p["subln_g"], 1 - lam_init, tq)
    sgu = spatial_gating(proj, p["sgu_ln_g"], p["sgu_ln_b"], p["w_spatial"], p["b_spatial"])
    conv = conformer_conv(proj, p["conv_w"], p["conv_b"], p["conv_ln_g"], p["conv_ln_b"])
    m = b * n
    y = merge_branches(h.reshape(m, d), att.reshape(m, -1), sgu.reshape(m, -1), conv.reshape(m, -1),
                       p["w_gate"], p["b_gate"], p["w_att_out"], p["w_sgu_out"], p["w_conv_out"])
    return proj_residual(y, p["w_o"], x.reshape(m, d), p["g1"], n).reshape(b, n, d)


def _channel(x, n, p):
    b, _, d = x.shape
    m = b * n
    x2 = x.reshape(m, d)
    if p["moe"]:
        h = normmod(x, p["norm2_g"], p["sh2"], p["sc2"], F32).reshape(m, d)
        out = moe_residual(h, p["router_w"], p["router_b"], p["w1"], p["w3"], p["w2"], x2, p["g2"], n)
    else:
        h = normmod(x, p["norm2_g"], p["sh2"], p["sc2"], BF16).reshape(m, d)
        out = ffn_residual(h, p["w1"], p["w3"], p["w2"], x2, p["g2"], n)
    return out.reshape(b, n, d)


def kernel(x, c, ctx, c_ctx, w_mod, b_mod, norm1_g, norm2_g, w_in, lam_q1, lam_k1, lam_q2, lam_k2, subln_g, w_att_out, sgu_ln_g, sgu_ln_b, w_spatial, b_spatial, w_sgu_out, conv_w, conv_b, conv_ln_g, conv_ln_b, w_conv_out, w_gate, b_gate, w_o, ffn_w1, ffn_w3, ffn_w2, router_w, router_b, moe_w1, moe_w3, moe_w2, final_g):
    b, s, d = x.shape
    nc = ctx.shape[1]
    depth = w_in.shape[0]
    cos, sin = rope_tables(s)
    ones, zeros = jnp.ones((nc, LANES), F32), jnp.zeros((nc, LANES), F32)

    c_rows = jnp.concatenate([c, c_ctx[None, :], jnp.zeros((8 - b - 1, d), F32)], axis=0)
    mods = modulation(c_rows, w_mod, b_mod)

    tk_lat = ATT_KEY_TILE if (s + nc) % ATT_KEY_TILE == 0 else _tile(s + nc, 256)
    xc = ctx
    for i in range(depth):
        last = i == depth - 1
        lat = [t[:, None, :] for t in jnp.split(mods[i, :b], 6, axis=-1)]
        cx = [jnp.broadcast_to(t[:, None, :], (b, 1, d)) for t in jnp.split(mods[i, b:b + 1], 6, axis=-1)]
        lam_init = 0.8 - 0.6 * math.exp(-0.3 * i)
        lam = (jnp.exp(jnp.sum(lam_q1[i] * lam_k1[i]).astype(F32))
               - jnp.exp(jnp.sum(lam_q2[i] * lam_k2[i]).astype(F32)) + lam_init)
        shared = dict(
            norm2_g=norm2_g[i], subln_g=subln_g[i], sgu_ln_g=sgu_ln_g[i], sgu_ln_b=sgu_ln_b[i],
            w_spatial=w_spatial[i], b_spatial=b_spatial[i], conv_w=conv_w[i], conv_b=conv_b[i],
            conv_ln_g=conv_ln_g[i], conv_ln_b=conv_ln_b[i],
            w_gate=w_gate[i].astype(BF16), b_gate=b_gate[i], w_att_out=w_att_out[i].astype(BF16),
            w_sgu_out=w_sgu_out[i].astype(BF16), w_conv_out=w_conv_out[i].astype(BF16), w_o=w_o[i].astype(BF16),
            moe=i % 2 == 1)
        j = i // 2
        if i % 2 == 0:
            shared.update(w1=ffn_w1[j][None].astype(BF16), w3=ffn_w3[j][None].astype(BF16),
                          w2=ffn_w2[j][None].astype(BF16))
        else:
            shared.update(w1=moe_w1[j].astype(BF16), w3=moe_w3[j].astype(BF16), w2=moe_w2[j].astype(BF16),
                          router_w=router_w[j], router_b=router_b[j])
        p_lat = dict(shared, g1=lat[2], sh2=lat[3], sc2=lat[4], g2=lat[5])
        p_ctx = dict(shared, g1=cx[2], sh2=cx[3], sc2=cx[4], g2=cx[5])
        w_in_i = w_in[i].astype(BF16)

        h_lat = normmod(x, norm1_g[i], lat[0], lat[1], BF16)
        h_ctx = normmod(xc, norm1_g[i], cx[0], cx[1], BF16)
        proj_l = matmul(h_lat.reshape(b * s, d), w_in_i, BF16).reshape(b, s, IN_COLS)
        proj_c = matmul(h_ctx.reshape(b * nc, d), w_in_i, BF16).reshape(b, nc, IN_COLS)
        rope_l = rope_qk(proj_l, cos, sin)
        rope_c = rope_qk(proj_c, ones, zeros)
        k_all = jnp.concatenate([rope_l[1], rope_c[1]], axis=1)
        v_all = jnp.concatenate([proj_l[..., 2 * ATT_WIDTH:3 * ATT_WIDTH],
                                 proj_c[..., 2 * ATT_WIDTH:3 * ATT_WIDTH]], axis=1)
        x = _mixer(x, s, h_lat, proj_l, k_all, v_all, lam, lam_init, _tile(s, 512), tk_lat, rope_l, p_lat)
        if not last:
            xc = _mixer(xc, nc, h_ctx, proj_c, rope_c[1], proj_c[..., 2 * ATT_WIDTH:3 * ATT_WIDTH],
                        lam, lam_init, nc, nc, rope_c, p_ctx)

        x = _channel(x, s, p_lat)
        if not last:
            xc = _channel(xc, nc, p_ctx)

    zero = jnp.zeros((b, 1, d), F32)
    return normmod(x, final_g, zero, zero, F32)
```

```python
import functools
import math

import jax
import jax.numpy as jnp
from jax import lax
from jax.experimental import pallas as pl
from jax.experimental.pallas import tpu as pltpu

D_MODEL = 2048
BATCH = 2
SEQ = 16384
DEPTH = 4
GRID_W = 64
CTX_LEN = 256
ATT_HEADS = 8
ATT_QK_DIM = 64
ATT_V_DIM = 2 * ATT_QK_DIM
ATT_WIDTH = ATT_HEADS * ATT_V_DIM
ROPE_BASE = 10000.0
SGU_CHUNK = 128
SGU_GROUPS = 8
SGU_WIDTH = 1024
CONV_WIDTH = 1024
CONV_K = 31
IN_COLS = 3 * ATT_WIDTH + 2 * SGU_WIDTH + 2 * CONV_WIDTH
N_BRANCH = 3
FFN_DIM = 5632
N_EXPERTS = 8
TOP_K = 2
EXPERT_DIM = 2816
EPS = 1e-6

LANES = 128
ATT_KEY_TILE = 640
HALO = 16
VMEM_LIMIT = 56 * 1024 * 1024
BF16 = jnp.bfloat16
F32 = jnp.float32


def _params(*sem):
    return pltpu.CompilerParams(dimension_semantics=sem, vmem_limit_bytes=VMEM_LIMIT)


def _tile(n, pref):
    t = min(pref, n)
    while n % t:
        t //= 2
    return t


def _normmod_kernel(x_ref, g_ref, sh_ref, sc_ref, o_ref):
    x = x_ref[0]
    y = x * lax.rsqrt(jnp.mean(x * x, axis=-1, keepdims=True) + EPS)
    y = y * g_ref[...]
    o_ref[0] = (y * (1 + sc_ref[0]) + sh_ref[0]).astype(o_ref.dtype)


def normmod(x, g, shift, scale, out_dtype):
    b, n, d = x.shape
    tn = _tile(n, 512)
    return pl.pallas_call(
        _normmod_kernel,
        grid=(b, n // tn),
        in_specs=[
            pl.BlockSpec((1, tn, d), lambda bi, i: (bi, i, 0)),
            pl.BlockSpec((1, d), lambda bi, i: (0, 0)),
            pl.BlockSpec((1, 1, d), lambda bi, i: (bi, 0, 0)),
            pl.BlockSpec((1, 1, d), lambda bi, i: (bi, 0, 0)),
        ],
        out_specs=pl.BlockSpec((1, tn, d), lambda bi, i: (bi, i, 0)),
        out_shape=jax.ShapeDtypeStruct((b, n, d), out_dtype),
        compiler_params=_params("parallel", "parallel"),
        name="normmod",
    )(x, g.reshape(1, d), shift, scale)


def _mod_kernel(c_ref, w_ref, b_ref, o_ref):
    c = c_ref[...]
    a = (c * jax.nn.sigmoid(c)).astype(BF16)
    o_ref[0] = jnp.dot(a, w_ref[0].astype(BF16), preferred_element_type=F32) + b_ref[0]


def modulation(c_rows, w_mod, b_mod):
    r, d = c_rows.shape
    nl, _, n = w_mod.shape
    tn = _tile(n, 1024)
    return pl.pallas_call(
        _mod_kernel,
        grid=(nl, n // tn),
        in_specs=[
            pl.BlockSpec((r, d), lambda l, j: (0, 0)),
            pl.BlockSpec((1, d, tn), lambda l, j: (l, 0, j)),
            pl.BlockSpec((1, 1, tn), lambda l, j: (l, 0, j)),
        ],
        out_specs=pl.BlockSpec((1, r, tn), lambda l, j: (l, 0, j)),
        out_shape=jax.ShapeDtypeStruct((nl, r, n), F32),
        compiler_params=_params("parallel", "parallel"),
        name="modulation",
    )(c_rows, w_mod, b_mod.reshape(nl, 1, n))


def _mm_kernel(a_ref, b_ref, o_ref):
    o_ref[...] = jnp.dot(a_ref[...], b_ref[...], preferred_element_type=F32).astype(o_ref.dtype)


def matmul(a, b, out_dtype):
    m, k = a.shape
    n = b.shape[1]
    tm, tn = _tile(m, 512), _tile(n, 1024)
    return pl.pallas_call(
        _mm_kernel,
        grid=(m // tm, n // tn),
        in_specs=[
            pl.BlockSpec((tm, k), lambda i, j: (i, 0)),
            pl.BlockSpec((k, tn), lambda i, j: (0, j)),
        ],
        out_specs=pl.BlockSpec((tm, tn), lambda i, j: (i, j)),
        out_shape=jax.ShapeDtypeStruct((m, n), out_dtype),
        compiler_params=_params("parallel", "parallel"),
        name="matmul",
    )(a, b)


def _swap_halves(x):
    n = x.shape[-1]
    quarter = ATT_QK_DIM // 4
    lane = lax.broadcasted_iota(jnp.int32, x.shape, x.ndim - 1)
    up = pltpu.roll(x, n - quarter, x.ndim - 1)
    down = pltpu.roll(x, quarter, x.ndim - 1)
    return jnp.where(lane % (2 * quarter) < quarter, up, down)


def _rope_kernel(q_ref, k_ref, cos_ref, sin_ref, qo_ref, ko_ref):
    cos = cos_ref[...]
    sin = sin_ref[...]
    scale = ATT_QK_DIM ** -0.5 * math.log2(math.e)
    lane = lax.broadcasted_iota(jnp.int32, cos.shape, 1)
    first = lane < ATT_QK_DIM
    for h in range(ATT_HEADS):
        cols = slice(h * LANES, (h + 1) * LANES)
        q = q_ref[0, :, cols].astype(F32)
        k = k_ref[0, :, cols].astype(F32)
        qr = (q * cos + _swap_halves(q) * sin) * scale
        kr = k * cos + _swap_halves(k) * sin
        qo_ref[0, h, 0] = jnp.where(first, qr, 0.0).astype(qo_ref.dtype)
        qo_ref[0, h, 1] = jnp.where(first, 0.0, qr).astype(qo_ref.dtype)
        ko_ref[0, :, cols] = kr.astype(ko_ref.dtype)


def rope_qk(proj, cos, sin):
    b, n, _ = proj.shape
    tn = _tile(n, 512)
    w = ATT_WIDTH
    return pl.pallas_call(
        _rope_kernel,
        grid=(b, n // tn),
        in_specs=[
            pl.BlockSpec((1, tn, w), lambda bi, i: (bi, i, 0)),
            pl.BlockSpec((1, tn, w), lambda bi, i: (bi, i, 1)),
            pl.BlockSpec((tn, LANES), lambda bi, i: (i, 0)),
            pl.BlockSpec((tn, LANES), lambda bi, i: (i, 0)),
        ],
        out_specs=[
            pl.BlockSpec((1, ATT_HEADS, 2, tn, LANES), lambda bi, i: (bi, 0, 0, i, 0)),
            pl.BlockSpec((1, tn, w), lambda bi, i: (bi, i, 0)),
        ],
        out_shape=[
            jax.ShapeDtypeStruct((b, ATT_HEADS, 2, n, LANES), BF16),
            jax.ShapeDtypeStruct((b, n, w), BF16),
        ],
        compiler_params=_params("parallel", "parallel"),
        name="rope_qk",
    )(proj, proj, cos, sin)


def rope_tables(n):
    rows = n // GRID_W
    row = jnp.broadcast_to(jnp.arange(rows, dtype=jnp.int32)[:, None], (rows, GRID_W)).reshape(-1)
    col = jnp.broadcast_to(jnp.arange(GRID_W, dtype=jnp.int32)[None, :], (rows, GRID_W)).reshape(-1)
    half = ATT_QK_DIM // 2
    inv = ROPE_BASE ** (-jnp.arange(0, half, 2, dtype=F32) / half)
    ang_r = row.astype(F32)[:, None] * inv
    ang_c = col.astype(F32)[:, None] * inv
    cr, sr, cc, sc = jnp.cos(ang_r), jnp.sin(ang_r), jnp.cos(ang_c), jnp.sin(ang_c)
    cos = jnp.concatenate([cr, cr, cc, cc], axis=-1)
    sin = jnp.concatenate([-sr, sr, -sc, sc], axis=-1)
    reps = LANES // ATT_QK_DIM
    return jnp.tile(cos, (1, reps)), jnp.tile(sin, (1, reps))


def _attn_kernel(lam_ref, q_ref, kt_ref, v_ref, g_ref, o_ref, m_ref, acc_ref, alpha_ref, s_ref, p_ref,
                 *, nk, tk, tq, rb, post_scale):
    rows_all = 2 * tq

    def scores(j, slot):
        q = q_ref[0, 0].reshape(rows_all, LANES)
        s_ref[slot] = jnp.dot(q, kt_ref[0, 0, j], preferred_element_type=F32)

    def weighted_values(j, slot):
        v = v_ref[0, pl.ds(pl.multiple_of(j * tk, tk), tk), :]
        vo = jnp.concatenate([v, jnp.ones((tk, LANES), BF16)], axis=1)
        alpha = alpha_ref[slot]
        acc_ref[...] = (jnp.concatenate([alpha, alpha], axis=1) * acc_ref[...]
                        + jnp.dot(p_ref[slot], vo, preferred_element_type=F32))

    def softmax(slot):
        for r in range(rows_all // rb):
            rows = pl.ds(r * rb, rb)
            s = s_ref[slot, rows, :]
            m_old = m_ref[rows, :]
            m_new = jnp.maximum(m_old, jnp.max(s, axis=-1, keepdims=True))
            alpha_ref[slot, rows, :] = jnp.exp2(m_old - m_new)
            p_ref[slot, rows, :] = jnp.exp2(s - m_new[:, :1]).astype(BF16)
            m_ref[rows, :] = m_new

    def stage(j, slot, score_next=True):
        if score_next:
            scores(jnp.minimum(j + 1, nk - 1), 1 - slot)
        weighted_values(jnp.maximum(j - 1, 0), 1 - slot)
        softmax(slot)

    def pair(jj, carry):
        stage(2 * jj, 0)
        stage(2 * jj + 1, 1)
        return carry

    m_ref[...] = jnp.full(m_ref.shape, -jnp.inf, F32)
    acc_ref[...] = jnp.zeros(acc_ref.shape, F32)
    p_ref[1] = jnp.zeros(p_ref.shape[1:], BF16)
    alpha_ref[1] = jnp.ones(alpha_ref.shape[1:], F32)
    scores(0, 0)
    lax.fori_loop(0, nk // 2, pair, 0)
    if nk % 2:
        stage(nk - 1, 0, score_next=False)
    weighted_values(nk - 1, (nk - 1) % 2)

    acc = acc_ref[...]
    o = acc[:, :LANES] / acc[:, LANES:]
    o = o[:tq] - lam_ref[0, 0] * o[tq:]
    o = o * lax.rsqrt(jnp.mean(o * o, axis=-1, keepdims=True) + EPS)
    o_ref[0] = (o * g_ref[...] * post_scale).astype(o_ref.dtype)


def diff_attention(q, kt, v, lam, subln_g, post_scale, tq):
    b, h, _, s, _ = q.shape
    nk, tk = kt.shape[2], kt.shape[4]
    nkeys = v.shape[1]
    rows_all = 2 * tq
    kern = functools.partial(_attn_kernel, nk=nk, tk=tk, tq=tq, rb=_tile(rows_all, 64), post_scale=post_scale)
    return pl.pallas_call(
        kern,
        grid=(b, h, s // tq),
        in_specs=[
            pl.BlockSpec(memory_space=pltpu.SMEM),
            pl.BlockSpec((1, 1, 2, tq, LANES), lambda bi, hi, i: (bi, hi, 0, i, 0)),
            pl.BlockSpec((1, 1, nk, LANES, tk), lambda bi, hi, i: (bi, hi, 0, 0, 0)),
            pl.BlockSpec((1, nkeys, LANES), lambda bi, hi, i: (bi, 0, hi)),
            pl.BlockSpec((1, LANES), lambda bi, hi, i: (0, 0)),
        ],
        out_specs=pl.BlockSpec((1, tq, LANES), lambda bi, hi, i: (bi, i, hi)),
        out_shape=jax.ShapeDtypeStruct((b, s, h * LANES), BF16),
        scratch_shapes=[
            pltpu.VMEM((rows_all, LANES), F32),
            pltpu.VMEM((rows_all, 2 * LANES), F32),
            pltpu.VMEM((2, rows_all, LANES), F32),
            pltpu.VMEM((2, rows_all, tk), F32),
            pltpu.VMEM((2, rows_all, tk), BF16),
        ],
        compiler_params=_params("parallel", "parallel", "parallel"),
        name="diff_attention",
    )(lam.reshape(1, 1), q, kt, v, subln_g.reshape(1, LANES))


def keys_transposed(k, tk):
    b, nkeys, _ = k.shape
    k = k.reshape(b, nkeys // tk, tk, ATT_HEADS, LANES)
    return k.transpose(0, 3, 1, 4, 2)


def _sgu_kernel(u_ref, v_ref, g_ref, b_ref, ws_ref, bs_ref, o_ref):
    v = jax.nn.gelu(v_ref[0].astype(F32))
    mu = jnp.mean(v, axis=-1, keepdims=True)
    var = jnp.mean(jnp.square(v - mu), axis=-1, keepdims=True)
    v = ((v - mu) * lax.rsqrt(var + EPS) * g_ref[...] + b_ref[...]).astype(BF16)
    tn = v.shape[0]
    cw = SGU_WIDTH // SGU_GROUPS
    for c in range(tn // SGU_CHUNK):
        rows = slice(c * SGU_CHUNK, (c + 1) * SGU_CHUNK)
        for g in range(SGU_GROUPS):
            cols = slice(g * cw, (g + 1) * cw)
            mixed = jnp.dot(ws_ref[g], v[rows, cols], preferred_element_type=F32) + bs_ref[g]
            u = jax.nn.gelu(u_ref[0, rows, cols].astype(F32))
            o_ref[0, rows, cols] = (u * mixed).astype(o_ref.dtype)


def spatial_gating(proj, ln_g, ln_b, w_s, b_s):
    b, n, _ = proj.shape
    tn = _tile(n, 512)
    w = SGU_WIDTH
    u_blk = 3 * ATT_WIDTH // w
    return pl.pallas_call(
        _sgu_kernel,
        grid=(b, n // tn),
        in_specs=[
            pl.BlockSpec((1, tn, w), lambda bi, i: (bi, i, u_blk)),
            pl.BlockSpec((1, tn, w), lambda bi, i: (bi, i, u_blk + 1)),
            pl.BlockSpec((1, w), lambda bi, i: (0, 0)),
            pl.BlockSpec((1, w), lambda bi, i: (0, 0)),
            pl.BlockSpec((SGU_GROUPS, SGU_CHUNK, SGU_CHUNK), lambda bi, i: (0, 0, 0)),
            pl.BlockSpec((SGU_GROUPS, SGU_CHUNK, 1), lambda bi, i: (0, 0, 0)),
        ],
        out_specs=pl.BlockSpec((1, tn, w), lambda bi, i: (bi, i, 0)),
        out_shape=jax.ShapeDtypeStruct((b, n, w), BF16),
        compiler_params=_params("parallel", "parallel"),
        name="spatial_gating",
    )(proj, proj, ln_g.reshape(1, w), ln_b.reshape(1, w), w_s.astype(BF16),
      b_s.reshape(SGU_GROUPS, SGU_CHUNK, 1))


def _glu(a_ref, g_ref):
    return a_ref[0].astype(F32) * jax.nn.sigmoid(g_ref[0].astype(F32))


def _conv_kernel(a_ref, g_ref, ap_ref, gp_ref, an_ref, gn_ref, w_ref, b_ref, lg_ref, lb_ref, o_ref, h_ref, *, rc):
    i = pl.program_id(1)
    tn = a_ref.shape[1]
    h_ref[pl.ds(HALO, tn), :] = _glu(a_ref, g_ref)
    h_ref[pl.ds(0, HALO), :] = jnp.where(i > 0, _glu(ap_ref, gp_ref), 0.0)
    h_ref[pl.ds(HALO + tn, HALO), :] = jnp.where(i < pl.num_programs(1) - 1, _glu(an_ref, gn_ref), 0.0)
    first = HALO - CONV_K // 2
    wrows = rc + 2 * HALO
    sub = 8

    def chunk(r, carry):
        r0 = pl.multiple_of(r * rc, rc)
        cols = []
        for c in range(CONV_WIDTH // LANES):
            lanes = pl.ds(c * LANES, LANES)
            win = h_ref[pl.ds(r0, wrows), lanes]
            acc = jnp.zeros((rc, LANES), F32) + b_ref[:, lanes]
            for rot in range(sub):
                shifted = win if rot == 0 else pltpu.roll(win, wrows - rot, 0)
                for k in range(CONV_K):
                    off = first + k
                    if off % sub == rot:
                        base = off - rot
                        acc = acc + w_ref[pl.ds(k, 1), lanes] * shifted[base:base + rc]
            cols.append(acc)
        acc = jnp.concatenate(cols, axis=1)
        mu = jnp.mean(acc, axis=-1, keepdims=True)
        var = jnp.mean(jnp.square(acc - mu), axis=-1, keepdims=True)
        y = (acc - mu) * lax.rsqrt(var + EPS) * lg_ref[...] + lb_ref[...]
        o_ref[0, pl.ds(r0, rc), :] = (y * jax.nn.sigmoid(y)).astype(o_ref.dtype)
        return carry

    lax.fori_loop(0, tn // rc, chunk, 0)


def conformer_conv(proj, w_dw, b_dw, ln_g, ln_b):
    b, n, _ = proj.shape
    tn = _tile(n, 512)
    w = CONV_WIDTH
    a_blk = (3 * ATT_WIDTH + 2 * SGU_WIDTH) // w
    nh = n // HALO
    per = tn // HALO

    def main(c):
        return pl.BlockSpec((1, tn, w), lambda bi, i: (bi, i, c))

    def prev(c):
        return pl.BlockSpec((1, HALO, w), lambda bi, i: (bi, jnp.maximum(i * per - 1, 0), c))

    def nxt(c):
        return pl.BlockSpec((1, HALO, w), lambda bi, i: (bi, jnp.minimum((i + 1) * per, nh - 1), c))

    vec = pl.BlockSpec((1, w), lambda bi, i: (0, 0))
    return pl.pallas_call(
        functools.partial(_conv_kernel, rc=_tile(tn, 64)),
        grid=(b, n // tn),
        in_specs=[main(a_blk), main(a_blk + 1), prev(a_blk), prev(a_blk + 1), nxt(a_blk), nxt(a_blk + 1),
                  pl.BlockSpec((CONV_K, w), lambda bi, i: (0, 0)), vec, vec, vec],
        out_specs=pl.BlockSpec((1, tn, w), lambda bi, i: (bi, i, 0)),
        out_shape=jax.ShapeDtypeStruct((b, n, w), BF16),
        scratch_shapes=[pltpu.VMEM((tn + 2 * HALO, w), F32)],
        compiler_params=_params("parallel", "parallel"),
        name="conformer_conv",
    )(proj, proj, proj, proj, proj, proj, w_dw, b_dw.reshape(1, w), ln_g.reshape(1, w), ln_b.reshape(1, w))


def _merge_kernel(h_ref, att_ref, sgu_ref, conv_ref, wg0, wg1, wg2, bg0, bg1, bg2, wa, ws, wc, o_ref):
    h = h_ref[...]
    y = None
    for br_ref, wg, bg, wo in ((att_ref, wg0, bg0, wa), (sgu_ref, wg1, bg1, ws), (conv_ref, wg2, bg2, wc)):
        gate = jax.nn.sigmoid(jnp.dot(h, wg[...], preferred_element_type=F32) + bg[...])
        t = gate * jnp.dot(br_ref[...], wo[...], preferred_element_type=F32)
        y = t if y is None else y + t
    o_ref[...] = y.astype(o_ref.dtype)


def merge_branches(h, att, sgu, conv, w_gate, b_gate, w_att_out, w_sgu_out, w_conv_out):
    m, d = h.shape
    tm, tn = _tile(m, 512), _tile(d, 512)
    nj = d // tn
    bw = att.shape[1]

    def wg(br):
        return pl.BlockSpec((d, tn), lambda i, j: (0, br * nj + j))

    def bg(br):
        return pl.BlockSpec((1, tn), lambda i, j: (0, br * nj + j))

    row = pl.BlockSpec((tm, bw), lambda i, j: (i, 0))
    wout = pl.BlockSpec((bw, tn), lambda i, j: (0, j))
    b_gate = b_gate.reshape(1, N_BRANCH * d)
    return pl.pallas_call(
        _merge_kernel,
        grid=(m // tm, nj),
        in_specs=[pl.BlockSpec((tm, d), lambda i, j: (i, 0)), row, row, row,
                  wg(0), wg(1), wg(2), bg(0), bg(1), bg(2), wout, wout, wout],
        out_specs=pl.BlockSpec((tm, tn), lambda i, j: (i, j)),
        out_shape=jax.ShapeDtypeStruct((m, d), BF16),
        compiler_params=_params("parallel", "parallel"),
        name="merge_branches",
    )(h, att, sgu, conv, w_gate, w_gate, w_gate, b_gate, b_gate, b_gate, w_att_out, w_sgu_out, w_conv_out)


def _proj_res_kernel(y_ref, w_ref, x_ref, g_ref, o_ref):
    o_ref[...] = x_ref[...] + g_ref[0] * jnp.dot(y_ref[...], w_ref[...], preferred_element_type=F32)


def proj_residual(y, w, x, gate, rows_per_batch):
    m, k = y.shape
    d = w.shape[1]
    tm = _tile(rows_per_batch, 512)
    per = rows_per_batch // tm
    return pl.pallas_call(
        _proj_res_kernel,
        grid=(m // tm,),
        in_specs=[
            pl.BlockSpec((tm, k), lambda i: (i, 0)),
            pl.BlockSpec((k, d), lambda i: (0, 0)),
            pl.BlockSpec((tm, d), lambda i: (i, 0)),
            pl.BlockSpec((1, 1, d), lambda i: (i // per, 0, 0)),
        ],
        out_specs=pl.BlockSpec((tm, d), lambda i: (i, 0)),
        out_shape=jax.ShapeDtypeStruct((m, d), F32),
        compiler_params=_params("parallel"),
        name="proj_residual",
    )(y, w, x, gate)


def _swiglu_hidden(h, w1_ref, w3_ref):
    a = jnp.dot(h, w1_ref[0], preferred_element_type=F32)
    b = jnp.dot(h, w3_ref[0], preferred_element_type=F32)
    return (a * jax.nn.sigmoid(a) * b).astype(BF16)


def _ffn_kernel(h_ref, w1_ref, w3_ref, w2_ref, x_ref, g_ref, o_ref, acc_ref):
    f = pl.program_id(1)
    y = jnp.dot(_swiglu_hidden(h_ref[...], w1_ref, w3_ref), w2_ref[0], preferred_element_type=F32)

    @pl.when(f == 0)
    def _():
        acc_ref[...] = y

    @pl.when(f > 0)
    def _():
        acc_ref[...] += y

    @pl.when(f == pl.num_programs(1) - 1)
    def _():
        o_ref[...] = x_ref[...] + g_ref[0] * acc_ref[...]


def ffn_residual(h, w1, w3, w2, x, gate, rows_per_batch):
    m, d = h.shape
    fdim = w1.shape[2]
    tm = _tile(rows_per_batch, 512)
    tf = _tile(fdim, 512)
    per = rows_per_batch // tm
    return pl.pallas_call(
        _ffn_kernel,
        grid=(m // tm, fdim // tf),
        in_specs=[
            pl.BlockSpec((tm, d), lambda i, f: (i, 0)),
            pl.BlockSpec((1, d, tf), lambda i, f: (0, 0, f)),
            pl.BlockSpec((1, d, tf), lambda i, f: (0, 0, f)),
            pl.BlockSpec((1, tf, d), lambda i, f: (0, f, 0)),
            pl.BlockSpec((tm, d), lambda i, f: (i, 0)),
            pl.BlockSpec((1, 1, d), lambda i, f: (i // per, 0, 0)),
        ],
        out_specs=pl.BlockSpec((tm, d), lambda i, f: (i, 0)),
        out_shape=jax.ShapeDtypeStruct((m, d), F32),
        scratch_shapes=[pltpu.VMEM((tm, d), F32)],
        compiler_params=_params("parallel", "arbitrary"),
        name="ffn_residual",
    )(h, w1, w3, w2, x, gate)


def _router_kernel(h_ref, w_ref, b_ref, sel_ref, wts_ref):
    logits = jnp.dot(h_ref[...].astype(BF16), w_ref[...], preferred_element_type=F32) + b_ref[...]
    ne = logits.shape[-1]
    idx = lax.broadcasted_iota(jnp.int32, logits.shape, 1)
    v1 = jnp.max(logits, axis=-1, keepdims=True)
    i1 = jnp.min(jnp.where(logits == v1, idx, ne), axis=-1, keepdims=True)
    rest = jnp.where(idx == i1, -jnp.inf, logits)
    v2 = jnp.max(rest, axis=-1, keepdims=True)
    i2 = jnp.min(jnp.where(rest == v2, idx, ne), axis=-1, keepdims=True)
    e2 = jnp.exp(v2 - v1)
    sel_ref[...] = jnp.concatenate([i1, i2], axis=1)
    wts_ref[...] = jnp.concatenate([1.0 / (1.0 + e2), e2 / (1.0 + e2)], axis=1)


def router(h, w_r, b_r):
    m, d = h.shape
    ne = w_r.shape[1]
    tm = _tile(m, 512)
    return pl.pallas_call(
        _router_kernel,
        grid=(m // tm,),
        in_specs=[
            pl.BlockSpec((tm, d), lambda i: (i, 0)),
            pl.BlockSpec((d, ne), lambda i: (0, 0)),
            pl.BlockSpec((1, ne), lambda i: (0, 0)),
        ],
        out_specs=[pl.BlockSpec((tm, TOP_K), lambda i: (i, 0)), pl.BlockSpec((tm, TOP_K), lambda i: (i, 0))],
        out_shape=[jax.ShapeDtypeStruct((m, TOP_K), jnp.int32), jax.ShapeDtypeStruct((m, TOP_K), F32)],
        compiler_params=_params("parallel"),
        name="router",
    )(h, w_r.astype(BF16), b_r.reshape(1, ne))


def dispatch_plan(sel, tm):
    m = sel.shape[0]
    ne = N_EXPERTS
    npairs = m * TOP_K
    rows = npairs + ne * tm
    e_flat = sel.reshape(npairs)
    onehot = (e_flat[:, None] == jnp.arange(ne, dtype=jnp.int32)[None, :]).astype(jnp.int32)
    rank = jnp.cumsum(onehot, axis=0) - onehot
    counts = jnp.sum(onehot, axis=0)
    padded = (counts + tm - 1) // tm * tm
    ends = jnp.cumsum(padded)
    pos = (ends - padded)[e_flat] + jnp.sum(rank * onehot, axis=1)
    tile_start = jnp.arange(rows // tm, dtype=jnp.int32) * tm
    tile_expert = jnp.minimum(jnp.searchsorted(ends, tile_start, side="right"), ne - 1).astype(jnp.int32)
    meta = jnp.concatenate([tile_expert, (ends[-1:] // tm).astype(jnp.int32)])
    return pos.reshape(m, TOP_K), meta, rows


def _wait_rows(src_ref, dst_ref, sem, n):
    pltpu.make_async_copy(src_ref.at[pl.ds(0, n)], dst_ref.at[pl.ds(0, n)], sem).wait()


def _scatter_rows_kernel(pos_ref, h_ref, init_ref, o_ref, sem):
    del init_ref
    tm = pos_ref.shape[2]
    base = pl.program_id(0) * tm

    def issue(r, carry):
        for k in range(TOP_K):
            pltpu.make_async_copy(h_ref.at[pl.ds(base + r, 1)], o_ref.at[pl.ds(pos_ref[0, k, r], 1)], sem).start()
        return carry

    lax.fori_loop(0, tm, issue, 0, unroll=8)
    _wait_rows(h_ref, o_ref, sem, TOP_K * tm)


def scatter_rows(h, pos, rows, tm):
    m, d = h.shape
    pos_t = pos.reshape(m // tm, tm, TOP_K).transpose(0, 2, 1)
    return pl.pallas_call(
        _scatter_rows_kernel,
        grid=(m // tm,),
        in_specs=[
            pl.BlockSpec((1, TOP_K, tm), lambda i: (i, 0, 0), memory_space=pltpu.SMEM),
            pl.BlockSpec(memory_space=pl.ANY),
            pl.BlockSpec(memory_space=pl.ANY),
        ],
        out_specs=pl.BlockSpec(memory_space=pl.ANY),
        out_shape=jax.ShapeDtypeStruct((rows, d), h.dtype),
        scratch_shapes=[pltpu.SemaphoreType.DMA(())],
        input_output_aliases={2: 0},
        compiler_params=_params("arbitrary"),
        name="scatter_rows",
    )(pos_t, h, jnp.zeros((rows, d), h.dtype))


def _expert_ffn_kernel(meta_ref, h_ref, w1_ref, w3_ref, w2_ref, o_ref, hb_ref):
    i, f = pl.program_id(0), pl.program_id(1)
    used = i < meta_ref[meta_ref.shape[0] - 1]

    @pl.when(f == 0)
    def _():
        hb_ref[...] = h_ref[...].astype(BF16)
        o_ref[...] = jnp.zeros(o_ref.shape, o_ref.dtype)

    @pl.when(used)
    def _():
        o_ref[...] += jnp.dot(_swiglu_hidden(hb_ref[...], w1_ref, w3_ref), w2_ref[0], preferred_element_type=F32)


def expert_ffn(hs, w1, w3, w2, meta, tm):
    r, d = hs.shape
    fdim = w1.shape[2]
    tf = _tile(fdim, 256)
    grid_spec = pltpu.PrefetchScalarGridSpec(
        num_scalar_prefetch=1,
        grid=(r // tm, fdim // tf),
        in_specs=[
            pl.BlockSpec((tm, d), lambda i, f, meta: (i, 0)),
            pl.BlockSpec((1, d, tf), lambda i, f, meta: (meta[i], 0, f)),
            pl.BlockSpec((1, d, tf), lambda i, f, meta: (meta[i], 0, f)),
            pl.BlockSpec((1, tf, d), lambda i, f, meta: (meta[i], f, 0)),
        ],
        out_specs=pl.BlockSpec((tm, d), lambda i, f, meta: (i, 0)),
        scratch_shapes=[pltpu.VMEM((tm, d), BF16)],
    )
    return pl.pallas_call(
        _expert_ffn_kernel,
        grid_spec=grid_spec,
        out_shape=jax.ShapeDtypeStruct((r, d), F32),
        compiler_params=_params("parallel", "arbitrary"),
        name="expert_ffn",
    )(meta, hs, w1, w3, w2)


def _combine_kernel(pos_ref, ys_ref, w_ref, x_ref, g_ref, o_ref, buf_ref, sem):
    tm = x_ref.shape[0]

    def issue(r, carry):
        for k in range(TOP_K):
            pltpu.make_async_copy(ys_ref.at[pl.ds(pos_ref[0, k, r], 1)], buf_ref.at[pl.ds(k * tm + r, 1)], sem).start()
        return carry

    lax.fori_loop(0, tm, issue, 0, unroll=8)
    _wait_rows(ys_ref, buf_ref, sem, TOP_K * tm)
    w = w_ref[...]
    y = w[:, 0:1] * buf_ref[pl.ds(0, tm), :]
    for k in range(1, TOP_K):
        y = y + w[:, k:k + 1] * buf_ref[pl.ds(k * tm, tm), :]
    o_ref[...] = x_ref[...] + g_ref[0] * y


def combine_residual(ys, pos, wts, x, gate, rows_per_batch):
    m, d = x.shape
    tm = _tile(rows_per_batch, 256)
    per = rows_per_batch // tm
    pos_t = pos.reshape(m // tm, tm, TOP_K).transpose(0, 2, 1)
    return pl.pallas_call(
        _combine_kernel,
        grid=(m // tm,),
        in_specs=[
            pl.BlockSpec((1, TOP_K, tm), lambda i: (i, 0, 0), memory_space=pltpu.SMEM),
            pl.BlockSpec(memory_space=pl.ANY),
            pl.BlockSpec((tm, TOP_K), lambda i: (i, 0)),
            pl.BlockSpec((tm, d), lambda i: (i, 0)),
            pl.BlockSpec((1, 1, d), lambda i: (i // per, 0, 0)),
        ],
        out_specs=pl.BlockSpec((tm, d), lambda i: (i, 0)),
        out_shape=jax.ShapeDtypeStruct((m, d), F32),
        scratch_shapes=[pltpu.VMEM((TOP_K * tm, d), F32), pltpu.SemaphoreType.DMA(())],
        compiler_params=_params("arbitrary"),
        name="combine_residual",
    )(pos_t, ys, wts, x, gate)


def moe_residual(h, w_r, b_r, w1, w3, w2, x, gate, rows_per_batch):
    m = h.shape[0]
    tm = 1024 if m * TOP_K >= 8 * 1024 else 128
    sel, wts = router(h, w_r, b_r)
    pos, meta, rows = dispatch_plan(sel, tm)
    hs = scatter_rows(h, pos, rows, _tile(rows_per_batch, 512))
    ys = expert_ffn(hs, w1, w3, w2, meta, tm)
    return combine_residual(ys, pos, wts, x, gate, rows_per_batch)


def _mixer(x, n, h, proj, k_all, v_all, lam, lam_init, tq, tk, rope, p):
    b = x.shape[0]
    d = x.shape[-1]
    q, _ = rope
    att = diff_attention(q, keys_transposed(k_all, tk), v_all, lam, p["subln_g"], 1 - lam_init, tq)
    sgu = spatial_gating(proj, p["sgu_ln_g"], p["sgu_ln_b"], p["w_spatial"], p["b_spatial"])
    conv = conformer_conv(proj, p["conv_w"], p["conv_b"], p["conv_ln_g"], p["conv_ln_b"])
    m = b * n
    y = merge_branches(h.reshape(m, d), att.reshape(m, -1), sgu.reshape(m, -1), conv.reshape(m, -1),
                       p["w_gate"], p["b_gate"], p["w_att_out"], p["w_sgu_out"], p["w_conv_out"])
    return proj_residual(y, p["w_o"], x.reshape(m, d), p["g1"], n).reshape(b, n, d)


def _channel(x, n, p):
    b, _, d = x.shape
    m = b * n
    x2 = x.reshape(m, d)
    if p["moe"]:
        h = normmod(x, p["norm2_g"], p["sh2"], p["sc2"], F32).reshape(m, d)
        out = moe_residual(h, p["router_w"], p["router_b"], p["w1"], p["w3"], p["w2"], x2, p["g2"], n)
    else:
        h = normmod(x, p["norm2_g"], p["sh2"], p["sc2"], BF16).reshape(m, d)
        out = ffn_residual(h, p["w1"], p["w3"], p["w2"], x2, p["g2"], n)
    return out.reshape(b, n, d)


def kernel(x, c, ctx, c_ctx, w_mod, b_mod, norm1_g, norm2_g, w_in, lam_q1, lam_k1, lam_q2, lam_k2, subln_g, w_att_out, sgu_ln_g, sgu_ln_b, w_spatial, b_spatial, w_sgu_out, conv_w, conv_b, conv_ln_g, conv_ln_b, w_conv_out, w_gate, b_gate, w_o, ffn_w1, ffn_w3, ffn_w2, router_w, router_b, moe_w1, moe_w3, moe_w2, final_g):
    b, s, d = x.shape
    nc = ctx.shape[1]
    depth = w_in.shape[0]
    cos, sin = rope_tables(s)
    ones, zeros = jnp.ones((nc, LANES), F32), jnp.zeros((nc, LANES), F32)

    c_rows = jnp.concatenate([c, c_ctx[None, :], jnp.zeros((8 - b - 1, d), F32)], axis=0)
    mods = modulation(c_rows, w_mod, b_mod)

    tk_lat = ATT_KEY_TILE if (s + nc) % ATT_KEY_TILE == 0 else _tile(s + nc, 256)
    xc = ctx
    for i in range(depth):
        last = i == depth - 1
        lat = [t[:, None, :] for t in jnp.split(mods[i, :b], 6, axis=-1)]
        cx = [jnp.broadcast_to(t[:, None, :], (b, 1, d)) for t in jnp.split(mods[i, b:b + 1], 6, axis=-1)]
        lam_init = 0.8 - 0.6 * math.exp(-0.3 * i)
        lam = (jnp.exp(jnp.sum(lam_q1[i] * lam_k1[i]).astype(F32))
               - jnp.exp(jnp.sum(lam_q2[i] * lam_k2[i]).astype(F32)) + lam_init)
        shared = dict(
            norm2_g=norm2_g[i], subln_g=subln_g[i], sgu_ln_g=sgu_ln_g[i], sgu_ln_b=sgu_ln_b[i],
            w_spatial=w_spatial[i], b_spatial=b_spatial[i], conv_w=conv_w[i], conv_b=conv_b[i],
            conv_ln_g=conv_ln_g[i], conv_ln_b=conv_ln_b[i],
            w_gate=w_gate[i].astype(BF16), b_gate=b_gate[i], w_att_out=w_att_out[i].astype(BF16),
            w_sgu_out=w_sgu_out[i].astype(BF16), w_conv_out=w_conv_out[i].astype(BF16), w_o=w_o[i].astype(BF16),
            moe=i % 2 == 1)
        j = i // 2
        if i % 2 == 0:
            shared.update(w1=ffn_w1[j][None].astype(BF16), w3=ffn_w3[j][None].astype(BF16),
                          w2=ffn_w2[j][None].astype(BF16))
        else:
            shared.update(w1=moe_w1[j].astype(BF16), w3=moe_w3[j].astype(BF16), w2=moe_w2[j].astype(BF16),
                          router_w=router_w[j], router_b=router_b[j])
        p_lat = dict(shared, g1=lat[2], sh2=lat[3], sc2=lat[4], g2=lat[5])
        p_ctx = dict(shared, g1=cx[2], sh2=cx[3], sc2=cx[4], g2=cx[5])
        w_in_i = w_in[i].astype(BF16)

        h_lat = normmod(x, norm1_g[i], lat[0], lat[1], BF16)
        h_ctx = normmod(xc, norm1_g[i], cx[0], cx[1], BF16)
        proj_l = matmul(h_lat.reshape(b * s, d), w_in_i, BF16).reshape(b, s, IN_COLS)
        proj_c = matmul(h_ctx.reshape(b * nc, d), w_in_i, BF16).reshape(b, nc, IN_COLS)
        rope_l = rope_qk(proj_l, cos, sin)
        rope_c = rope_qk(proj_c, ones, zeros)
        k_all = jnp.concatenate([rope_l[1], rope_c[1]], axis=1)
        v_all = jnp.concatenate([proj_l[..., 2 * ATT_WIDTH:3 * ATT_WIDTH],
                                 proj_c[..., 2 * ATT_WIDTH:3 * ATT_WIDTH]], axis=1)
        x = _mixer(x, s, h_lat, proj_l, k_all, v_all, lam, lam_init, _tile(s, 512), tk_lat, rope_l, p_lat)
        if not last:
            xc = _mixer(xc, nc, h_ctx, proj_c, rope_c[1], proj_c[..., 2 * ATT_WIDTH:3 * ATT_WIDTH],
                        lam, lam_init, nc, nc, rope_c, p_ctx)

        x = _channel(x, s, p_lat)
        if not last:
            xc = _channel(xc, nc, p_ctx)

    zero = jnp.zeros((b, 1, d), F32)
    return normmod(x, final_g, zero, zero, F32)
```

```python
import functools
import math

import jax
import jax.numpy as jnp
from jax import lax
from jax.experimental import pallas as pl
from jax.experimental.pallas import tpu as pltpu

D_MODEL = 2048
BATCH = 2
SEQ = 16384
DEPTH = 4
GRID_W = 64
CTX_LEN = 256
ATT_HEADS = 8
ATT_QK_DIM = 64
ATT_V_DIM = 2 * ATT_QK_DIM
ATT_WIDTH = ATT_HEADS * ATT_V_DIM
ROPE_BASE = 10000.0
SGU_CHUNK = 128
SGU_GROUPS = 8
SGU_WIDTH = 1024
CONV_WIDTH = 1024
CONV_K = 31
IN_COLS = 3 * ATT_WIDTH + 2 * SGU_WIDTH + 2 * CONV_WIDTH
N_BRANCH = 3
FFN_DIM = 5632
N_EXPERTS = 8
TOP_K = 2
EXPERT_DIM = 2816
EPS = 1e-6

LANES = 128
ATT_KEY_TILE = 640
HALO = 16
VMEM_LIMIT = 56 * 1024 * 1024
BF16 = jnp.bfloat16
F32 = jnp.float32


def _params(*sem):
    return pltpu.CompilerParams(dimension_semantics=sem, vmem_limit_bytes=VMEM_LIMIT)


def _tile(n, pref):
    t = min(pref, n)
    while n % t:
        t //= 2
    return t


def _normmod_kernel(x_ref, g_ref, sh_ref, sc_ref, o_ref):
    x = x_ref[0]
    y = x * lax.rsqrt(jnp.mean(x * x, axis=-1, keepdims=True) + EPS)
    y = y * g_ref[...]
    o_ref[0] = (y * (1 + sc_ref[0]) + sh_ref[0]).astype(o_ref.dtype)


def normmod(x, g, shift, scale, out_dtype):
    b, n, d = x.shape
    tn = _tile(n, 512)
    return pl.pallas_call(
        _normmod_kernel,
        grid=(b, n // tn),
        in_specs=[
            pl.BlockSpec((1, tn, d), lambda bi, i: (bi, i, 0)),
            pl.BlockSpec((1, d), lambda bi, i: (0, 0)),
            pl.BlockSpec((1, 1, d), lambda bi, i: (bi, 0, 0)),
            pl.BlockSpec((1, 1, d), lambda bi, i: (bi, 0, 0)),
        ],
        out_specs=pl.BlockSpec((1, tn, d), lambda bi, i: (bi, i, 0)),
        out_shape=jax.ShapeDtypeStruct((b, n, d), out_dtype),
        compiler_params=_params("parallel", "parallel"),
        name="normmod",
    )(x, g.reshape(1, d), shift, scale)


def _mod_kernel(c_ref, w_ref, b_ref, o_ref):
    c = c_ref[...]
    a = (c * jax.nn.sigmoid(c)).astype(BF16)
    o_ref[0] = jnp.dot(a, w_ref[0].astype(BF16), preferred_element_type=F32) + b_ref[0]


def modulation(c_rows, w_mod, b_mod):
    r, d = c_rows.shape
    nl, _, n = w_mod.shape
    tn = _tile(n, 1024)
    return pl.pallas_call(
        _mod_kernel,
        grid=(nl, n // tn),
        in_specs=[
            pl.BlockSpec((r, d), lambda l, j: (0, 0)),
            pl.BlockSpec((1, d, tn), lambda l, j: (l, 0, j)),
            pl.BlockSpec((1, 1, tn), lambda l, j: (l, 0, j)),
        ],
        out_specs=pl.BlockSpec((1, r, tn), lambda l, j: (l, 0, j)),
        out_shape=jax.ShapeDtypeStruct((nl, r, n), F32),
        compiler_params=_params("parallel", "parallel"),
        name="modulation",
    )(c_rows, w_mod, b_mod.reshape(nl, 1, n))


def _mm_kernel(a_ref, b_ref, o_ref):
    o_ref[...] = jnp.dot(a_ref[...], b_ref[...], preferred_element_type=F32).astype(o_ref.dtype)


def matmul(a, b, out_dtype):
    m, k = a.shape
    n = b.shape[1]
    tm, tn = _tile(m, 512), _tile(n, 1024)
    return pl.pallas_call(
        _mm_kernel,
        grid=(m // tm, n // tn),
        in_specs=[
            pl.BlockSpec((tm, k), lambda i, j: (i, 0)),
            pl.BlockSpec((k, tn), lambda i, j: (0, j)),
        ],
        out_specs=pl.BlockSpec((tm, tn), lambda i, j: (i, j)),
        out_shape=jax.ShapeDtypeStruct((m, n), out_dtype),
        compiler_params=_params("parallel", "parallel"),
        name="matmul",
    )(a, b)


def _swap_halves(x):
    n = x.shape[-1]
    quarter = ATT_QK_DIM // 4
    lane = lax.broadcasted_iota(jnp.int32, x.shape, x.ndim - 1)
    up = pltpu.roll(x, n - quarter, x.ndim - 1)
    down = pltpu.roll(x, quarter, x.ndim - 1)
    return jnp.where(lane % (2 * quarter) < quarter, up, down)


def _rope_kernel(q_ref, k_ref, cos_ref, sin_ref, qo_ref, ko_ref):
    cos = cos_ref[...]
    sin = sin_ref[...]
    scale = ATT_QK_DIM ** -0.5 * math.log2(math.e)
    lane = lax.broadcasted_iota(jnp.int32, cos.shape, 1)
    first = lane < ATT_QK_DIM
    for h in range(ATT_HEADS):
        cols = slice(h * LANES, (h + 1) * LANES)
        q = q_ref[0, :, cols].astype(F32)
        k = k_ref[0, :, cols].astype(F32)
        qr = (q * cos + _swap_halves(q) * sin) * scale
        kr = k * cos + _swap_halves(k) * sin
        qo_ref[0, h, 0] = jnp.where(first, qr, 0.0).astype(qo_ref.dtype)
        qo_ref[0, h, 1] = jnp.where(first, 0.0, qr).astype(qo_ref.dtype)
        ko_ref[0, :, cols] = kr.astype(ko_ref.dtype)


def rope_qk(proj, cos, sin):
    b, n, _ = proj.shape
    tn = _tile(n, 512)
    w = ATT_WIDTH
    return pl.pallas_call(
        _rope_kernel,
        grid=(b, n // tn),
        in_specs=[
            pl.BlockSpec((1, tn, w), lambda bi, i: (bi, i, 0)),
            pl.BlockSpec((1, tn, w), lambda bi, i: (bi, i, 1)),
            pl.BlockSpec((tn, LANES), lambda bi, i: (i, 0)),
            pl.BlockSpec((tn, LANES), lambda bi, i: (i, 0)),
        ],
        out_specs=[
            pl.BlockSpec((1, ATT_HEADS, 2, tn, LANES), lambda bi, i: (bi, 0, 0, i, 0)),
            pl.BlockSpec((1, tn, w), lambda bi, i: (bi, i, 0)),
        ],
        out_shape=[
            jax.ShapeDtypeStruct((b, ATT_HEADS, 2, n, LANES), BF16),
            jax.ShapeDtypeStruct((b, n, w), BF16),
        ],
        compiler_params=_params("parallel", "parallel"),
        name="rope_qk",
    )(proj, proj, cos, sin)


def rope_tables(n):
    rows = n // GRID_W
    row = jnp.broadcast_to(jnp.arange(rows, dtype=jnp.int32)[:, None], (rows, GRID_W)).reshape(-1)
    col = jnp.broadcast_to(jnp.arange(GRID_W, dtype=jnp.int32)[None, :], (rows, GRID_W)).reshape(-1)
    half = ATT_QK_DIM // 2
    inv = ROPE_BASE ** (-jnp.arange(0, half, 2, dtype=F32) / half)
    ang_r = row.astype(F32)[:, None] * inv
    ang_c = col.astype(F32)[:, None] * inv
    cr, sr, cc, sc = jnp.cos(ang_r), jnp.sin(ang_r), jnp.cos(ang_c), jnp.sin(ang_c)
    cos = jnp.concatenate([cr, cr, cc, cc], axis=-1)
    sin = jnp.concatenate([-sr, sr, -sc, sc], axis=-1)
    reps = LANES // ATT_QK_DIM
    return jnp.tile(cos, (1, reps)), jnp.tile(sin, (1, reps))


def _attn_kernel(lam_ref, q_ref, kt_ref, v_ref, g_ref, o_ref, m_ref, acc_ref, alpha_ref, s_ref, p_ref,
                 *, nk, tk, tq, rb, last_valid, post_scale):
    rows_all = 2 * tq

    def scores(j, slot):
        q = q_ref[0, 0].reshape(rows_all, LANES)
        s_ref[slot] = jnp.dot(q, kt_ref[0, 0, j], preferred_element_type=F32)

    def weighted_values(j, slot):
        v = v_ref[0, pl.ds(pl.multiple_of(j * tk, tk), tk), :]
        vo = jnp.concatenate([v, jnp.ones((tk, LANES), BF16)], axis=1)
        alpha = alpha_ref[slot]
        acc_ref[...] = (jnp.concatenate([alpha, alpha], axis=1) * acc_ref[...]
                        + jnp.dot(p_ref[slot], vo, preferred_element_type=F32))

    def softmax(slot, valid=tk):
        for r in range(rows_all // rb):
            rows = pl.ds(r * rb, rb)
            s = s_ref[slot, rows, :]
            if valid < tk:
                s = jnp.where(lax.broadcasted_iota(jnp.int32, s.shape, 1) < valid, s, -jnp.inf)
            m_old = m_ref[rows, :]
            m_new = jnp.maximum(m_old, jnp.max(s, axis=-1, keepdims=True))
            alpha_ref[slot, rows, :] = jnp.exp2(m_old - m_new)
            p_ref[slot, rows, :] = jnp.exp2(s - m_new[:, :1]).astype(BF16)
            m_ref[rows, :] = m_new

    def stage(j, slot, first=False, last=False):
        if not last:
            scores(j + 1, 1 - slot)
        if not first:
            weighted_values(j - 1, 1 - slot)
        softmax(slot, last_valid if last else tk)

    def pair(jj, carry):
        stage(2 * jj + 1, 1)
        stage(2 * jj + 2, 0)
        return carry

    m_ref[...] = jnp.full(m_ref.shape, -jnp.inf, F32)
    acc_ref[...] = jnp.zeros(acc_ref.shape, F32)
    scores(0, 0)
    stage(0, 0, first=True, last=nk == 1)
    npairs = (nk - 2) // 2 if nk >= 2 else 0
    lax.fori_loop(0, npairs, pair, 0)
    for j in range(1 + 2 * npairs, nk):
        stage(j, j % 2, last=j == nk - 1)
    weighted_values(nk - 1, (nk - 1) % 2)

    acc = acc_ref[...]
    o = acc[:, :LANES] / acc[:, LANES:]
    o = o[:tq] - lam_ref[0, 0] * o[tq:]
    o = o * lax.rsqrt(jnp.mean(o * o, axis=-1, keepdims=True) + EPS)
    o_ref[0] = (o * g_ref[...] * post_scale).astype(o_ref.dtype)


def diff_attention(q, kt, v, n_valid, lam, subln_g, post_scale, tq):
    b, h, _, s, _ = q.shape
    nk, tk = kt.shape[2], kt.shape[4]
    nkeys = v.shape[1]
    rows_all = 2 * tq
    last_valid = n_valid - (nk - 1) * tk
    assert nkeys == nk * tk and 0 < last_valid <= tk
    kern = functools.partial(_attn_kernel, nk=nk, tk=tk, tq=tq, rb=_tile(rows_all, 64), last_valid=last_valid,
                             post_scale=post_scale)
    return pl.pallas_call(
        kern,
        grid=(b, h, s // tq),
        in_specs=[
            pl.BlockSpec(memory_space=pltpu.SMEM),
            pl.BlockSpec((1, 1, 2, tq, LANES), lambda bi, hi, i: (bi, hi, 0, i, 0)),
            pl.BlockSpec((1, 1, nk, LANES, tk), lambda bi, hi, i: (bi, hi, 0, 0, 0)),
            pl.BlockSpec((1, nkeys, LANES), lambda bi, hi, i: (bi, 0, hi)),
            pl.BlockSpec((1, LANES), lambda bi, hi, i: (0, 0)),
        ],
        out_specs=pl.BlockSpec((1, tq, LANES), lambda bi, hi, i: (bi, i, hi)),
        out_shape=jax.ShapeDtypeStruct((b, s, h * LANES), BF16),
        scratch_shapes=[
            pltpu.VMEM((rows_all, LANES), F32),
            pltpu.VMEM((rows_all, 2 * LANES), F32),
            pltpu.VMEM((2, rows_all, LANES), F32),
            pltpu.VMEM((2, rows_all, tk), F32),
            pltpu.VMEM((2, rows_all, tk), BF16),
        ],
        compiler_params=_params("parallel", "parallel", "parallel"),
        name="diff_attention",
    )(lam.reshape(1, 1), q, kt, v, subln_g.reshape(1, LANES))


def keys_transposed(k, tk):
    b, nkeys, _ = k.shape
    k = k.reshape(b, nkeys // tk, tk, ATT_HEADS, LANES)
    return k.transpose(0, 3, 1, 4, 2)


def _sgu_kernel(u_ref, v_ref, g_ref, b_ref, ws_ref, bs_ref, o_ref):
    v = jax.nn.gelu(v_ref[0].astype(F32))
    mu = jnp.mean(v, axis=-1, keepdims=True)
    var = jnp.mean(jnp.square(v - mu), axis=-1, keepdims=True)
    v = ((v - mu) * lax.rsqrt(var + EPS) * g_ref[...] + b_ref[...]).astype(BF16)
    tn = v.shape[0]
    cw = SGU_WIDTH // SGU_GROUPS
    for c in range(tn // SGU_CHUNK):
        rows = slice(c * SGU_CHUNK, (c + 1) * SGU_CHUNK)
        for g in range(SGU_GROUPS):
            cols = slice(g * cw, (g + 1) * cw)
            mixed = jnp.dot(ws_ref[g], v[rows, cols], preferred_element_type=F32) + bs_ref[g]
            u = jax.nn.gelu(u_ref[0, rows, cols].astype(F32))
            o_ref[0, rows, cols] = (u * mixed).astype(o_ref.dtype)


def spatial_gating(proj, ln_g, ln_b, w_s, b_s):
    b, n, _ = proj.shape
    tn = _tile(n, 512)
    w = SGU_WIDTH
    u_blk = 3 * ATT_WIDTH // w
    return pl.pallas_call(
        _sgu_kernel,
        grid=(b, n // tn),
        in_specs=[
            pl.BlockSpec((1, tn, w), lambda bi, i: (bi, i, u_blk)),
            pl.BlockSpec((1, tn, w), lambda bi, i: (bi, i, u_blk + 1)),
            pl.BlockSpec((1, w), lambda bi, i: (0, 0)),
            pl.BlockSpec((1, w), lambda bi, i: (0, 0)),
            pl.BlockSpec((SGU_GROUPS, SGU_CHUNK, SGU_CHUNK), lambda bi, i: (0, 0, 0)),
            pl.BlockSpec((SGU_GROUPS, SGU_CHUNK, 1), lambda bi, i: (0, 0, 0)),
        ],
        out_specs=pl.BlockSpec((1, tn, w), lambda bi, i: (bi, i, 0)),
        out_shape=jax.ShapeDtypeStruct((b, n, w), BF16),
        compiler_params=_params("parallel", "parallel"),
        name="spatial_gating",
    )(proj, proj, ln_g.reshape(1, w), ln_b.reshape(1, w), w_s.astype(BF16),
      b_s.reshape(SGU_GROUPS, SGU_CHUNK, 1))


def _glu(a_ref, g_ref):
    return a_ref[0].astype(F32) * jax.nn.sigmoid(g_ref[0].astype(F32))


def _conv_kernel(a_ref, g_ref, ap_ref, gp_ref, an_ref, gn_ref, w_ref, b_ref, lg_ref, lb_ref, o_ref, h_ref, *, rc):
    i = pl.program_id(1)
    tn = a_ref.shape[1]
    h_ref[pl.ds(HALO, tn), :] = _glu(a_ref, g_ref)
    h_ref[pl.ds(0, HALO), :] = jnp.where(i > 0, _glu(ap_ref, gp_ref), 0.0)
    h_ref[pl.ds(HALO + tn, HALO), :] = jnp.where(i < pl.num_programs(1) - 1, _glu(an_ref, gn_ref), 0.0)
    first = HALO - CONV_K // 2
    wrows = rc + 2 * HALO
    sub = 8

    def chunk(r, carry):
        r0 = pl.multiple_of(r * rc, rc)
        cols = []
        for c in range(CONV_WIDTH // LANES):
            lanes = pl.ds(c * LANES, LANES)
            win = h_ref[pl.ds(r0, wrows), lanes]
            acc = jnp.zeros((rc, LANES), F32) + b_ref[:, lanes]
            for rot in range(sub):
                shifted = win if rot == 0 else pltpu.roll(win, wrows - rot, 0)
                for k in range(CONV_K):
                    off = first + k
                    if off % sub == rot:
                        base = off - rot
                        acc = acc + w_ref[pl.ds(k, 1), lanes] * shifted[base:base + rc]
            cols.append(acc)
        acc = jnp.concatenate(cols, axis=1)
        mu = jnp.mean(acc, axis=-1, keepdims=True)
        var = jnp.mean(jnp.square(acc - mu), axis=-1, keepdims=True)
        y = (acc - mu) * lax.rsqrt(var + EPS) * lg_ref[...] + lb_ref[...]
        o_ref[0, pl.ds(r0, rc), :] = (y * jax.nn.sigmoid(y)).astype(o_ref.dtype)
        return carry

    lax.fori_loop(0, tn // rc, chunk, 0)


def conformer_conv(proj, w_dw, b_dw, ln_g, ln_b):
    b, n, _ = proj.shape
    tn = _tile(n, 512)
    w = CONV_WIDTH
    a_blk = (3 * ATT_WIDTH + 2 * SGU_WIDTH) // w
    nh = n // HALO
    per = tn // HALO

    def main(c):
        return pl.BlockSpec((1, tn, w), lambda bi, i: (bi, i, c))

    def prev(c):
        return pl.BlockSpec((1, HALO, w), lambda bi, i: (bi, jnp.maximum(i * per - 1, 0), c))

    def nxt(c):
        return pl.BlockSpec((1, HALO, w), lambda bi, i: (bi, jnp.minimum((i + 1) * per, nh - 1), c))

    vec = pl.BlockSpec((1, w), lambda bi, i: (0, 0))
    return pl.pallas_call(
        functools.partial(_conv_kernel, rc=_tile(tn, 64)),
        grid=(b, n // tn),
        in_specs=[main(a_blk), main(a_blk + 1), prev(a_blk), prev(a_blk + 1), nxt(a_blk), nxt(a_blk + 1),
                  pl.BlockSpec((CONV_K, w), lambda bi, i: (0, 0)), vec, vec, vec],
        out_specs=pl.BlockSpec((1, tn, w), lambda bi, i: (bi, i, 0)),
        out_shape=jax.ShapeDtypeStruct((b, n, w), BF16),
        scratch_shapes=[pltpu.VMEM((tn + 2 * HALO, w), F32)],
        compiler_params=_params("parallel", "parallel"),
        name="conformer_conv",
    )(proj, proj, proj, proj, proj, proj, w_dw, b_dw.reshape(1, w), ln_g.reshape(1, w), ln_b.reshape(1, w))


def _merge_kernel(h_ref, att_ref, sgu_ref, conv_ref, wg0, wg1, wg2, bg0, bg1, bg2, wa, ws, wc, o_ref):
    h = h_ref[...]
    y = None
    for br_ref, wg, bg, wo in ((att_ref, wg0, bg0, wa), (sgu_ref, wg1, bg1, ws), (conv_ref, wg2, bg2, wc)):
        gate = jax.nn.sigmoid(jnp.dot(h, wg[...], preferred_element_type=F32) + bg[...])
        t = gate * jnp.dot(br_ref[...], wo[...], preferred_element_type=F32)
        y = t if y is None else y + t
    o_ref[...] = y.astype(o_ref.dtype)


def merge_branches(h, att, sgu, conv, w_gate, b_gate, w_att_out, w_sgu_out, w_conv_out):
    m, d = h.shape
    tm, tn = _tile(m, 512), _tile(d, 512)
    nj = d // tn
    bw = att.shape[1]

    def wg(br):
        return pl.BlockSpec((d, tn), lambda i, j: (0, br * nj + j))

    def bg(br):
        return pl.BlockSpec((1, tn), lambda i, j: (0, br * nj + j))

    row = pl.BlockSpec((tm, bw), lambda i, j: (i, 0))
    wout = pl.BlockSpec((bw, tn), lambda i, j: (0, j))
    b_gate = b_gate.reshape(1, N_BRANCH * d)
    return pl.pallas_call(
        _merge_kernel,
        grid=(m // tm, nj),
        in_specs=[pl.BlockSpec((tm, d), lambda i, j: (i, 0)), row, row, row,
                  wg(0), wg(1), wg(2), bg(0), bg(1), bg(2), wout, wout, wout],
        out_specs=pl.BlockSpec((tm, tn), lambda i, j: (i, j)),
        out_shape=jax.ShapeDtypeStruct((m, d), BF16),
        compiler_params=_params("parallel", "parallel"),
        name="merge_branches",
    )(h, att, sgu, conv, w_gate, w_gate, w_gate, b_gate, b_gate, b_gate, w_att_out, w_sgu_out, w_conv_out)


def _proj_res_kernel(y_ref, w_ref, x_ref, g_ref, o_ref):
    o_ref[...] = x_ref[...] + g_ref[0] * jnp.dot(y_ref[...], w_ref[...], preferred_element_type=F32)


def proj_residual(y, w, x, gate, rows_per_batch):
    m, k = y.shape
    d = w.shape[1]
    tm = _tile(rows_per_batch, 512)
    per = rows_per_batch // tm
    return pl.pallas_call(
        _proj_res_kernel,
        grid=(m // tm,),
        in_specs=[
            pl.BlockSpec((tm, k), lambda i: (i, 0)),
            pl.BlockSpec((k, d), lambda i: (0, 0)),
            pl.BlockSpec((tm, d), lambda i: (i, 0)),
            pl.BlockSpec((1, 1, d), lambda i: (i // per, 0, 0)),
        ],
        out_specs=pl.BlockSpec((tm, d), lambda i: (i, 0)),
        out_shape=jax.ShapeDtypeStruct((m, d), F32),
        compiler_params=_params("parallel"),
        name="proj_residual",
    )(y, w, x, gate)


def _swiglu_out(h_ref, w1_ref, w3_ref, w2_ref, parts=1):
    rows = h_ref.shape[0] // parts
    outs = []
    for r in range(parts):
        h = h_ref[pl.ds(r * rows, rows), :]
        a = jnp.dot(h, w1_ref[0], preferred_element_type=F32)
        b = jnp.dot(h, w3_ref[0], preferred_element_type=F32)
        z = (a * jax.nn.sigmoid(a) * b).astype(BF16)
        outs.append(jnp.dot(z, w2_ref[0], preferred_element_type=F32))
    return jnp.concatenate(outs, axis=0)


def _ffn_kernel(h_ref, w1_ref, w3_ref, w2_ref, x_ref, g_ref, o_ref, acc_ref):
    f = pl.program_id(1)
    y = _swiglu_out(h_ref, w1_ref, w3_ref, w2_ref)

    @pl.when(f == 0)
    def _():
        acc_ref[...] = y

    @pl.when(f > 0)
    def _():
        acc_ref[...] += y

    @pl.when(f == pl.num_programs(1) - 1)
    def _():
        o_ref[...] = x_ref[...] + g_ref[0] * acc_ref[...]


def ffn_residual(h, w1, w3, w2, x, gate, rows_per_batch):
    m, d = h.shape
    fdim = w1.shape[2]
    tm = _tile(rows_per_batch, 512)
    tf = _tile(fdim, 512)
    per = rows_per_batch // tm
    return pl.pallas_call(
        _ffn_kernel,
        grid=(m // tm, fdim // tf),
        in_specs=[
            pl.BlockSpec((tm, d), lambda i, f: (i, 0)),
            pl.BlockSpec((1, d, tf), lambda i, f: (0, 0, f)),
            pl.BlockSpec((1, d, tf), lambda i, f: (0, 0, f)),
            pl.BlockSpec((1, tf, d), lambda i, f: (0, f, 0)),
            pl.BlockSpec((tm, d), lambda i, f: (i, 0)),
            pl.BlockSpec((1, 1, d), lambda i, f: (i // per, 0, 0)),
        ],
        out_specs=pl.BlockSpec((tm, d), lambda i, f: (i, 0)),
        out_shape=jax.ShapeDtypeStruct((m, d), F32),
        scratch_shapes=[pltpu.VMEM((tm, d), F32)],
        compiler_params=_params("parallel", "arbitrary"),
        name="ffn_residual",
    )(h, w1, w3, w2, x, gate)


def _router_kernel(h_ref, w_ref, b_ref, sel_ref, wts_ref):
    logits = jnp.dot(h_ref[...].astype(BF16), w_ref[...], preferred_element_type=F32) + b_ref[...]
    ne = logits.shape[-1]
    idx = lax.broadcasted_iota(jnp.int32, logits.shape, 1)
    v1 = jnp.max(logits, axis=-1, keepdims=True)
    i1 = jnp.min(jnp.where(logits == v1, idx, ne), axis=-1, keepdims=True)
    rest = jnp.where(idx == i1, -jnp.inf, logits)
    v2 = jnp.max(rest, axis=-1, keepdims=True)
    i2 = jnp.min(jnp.where(rest == v2, idx, ne), axis=-1, keepdims=True)
    e2 = jnp.exp(v2 - v1)
    sel_ref[...] = jnp.concatenate([i1, i2], axis=1)
    wts_ref[...] = jnp.concatenate([1.0 / (1.0 + e2), e2 / (1.0 + e2)], axis=1)


def router(h, w_r, b_r):
    m, d = h.shape
    ne = w_r.shape[1]
    tm = _tile(m, 512)
    return pl.pallas_call(
        _router_kernel,
        grid=(m // tm,),
        in_specs=[
            pl.BlockSpec((tm, d), lambda i: (i, 0)),
            pl.BlockSpec((d, ne), lambda i: (0, 0)),
            pl.BlockSpec((1, ne), lambda i: (0, 0)),
        ],
        out_specs=[pl.BlockSpec((tm, TOP_K), lambda i: (i, 0)), pl.BlockSpec((tm, TOP_K), lambda i: (i, 0))],
        out_shape=[jax.ShapeDtypeStruct((m, TOP_K), jnp.int32), jax.ShapeDtypeStruct((m, TOP_K), F32)],
        compiler_params=_params("parallel"),
        name="router",
    )(h, w_r.astype(BF16), b_r.reshape(1, ne))


def dispatch_plan(sel, tm):
    m = sel.shape[0]
    ne = N_EXPERTS
    npairs = m * TOP_K
    rows = npairs + ne * tm
    e_flat = sel.reshape(npairs)
    onehot = (e_flat[:, None] == jnp.arange(ne, dtype=jnp.int32)[None, :]).astype(jnp.int32)
    rank = jnp.cumsum(onehot, axis=0) - onehot
    counts = jnp.sum(onehot, axis=0)
    padded = (counts + tm - 1) // tm * tm
    ends = jnp.cumsum(padded)
    pos = (ends - padded)[e_flat] + jnp.sum(rank * onehot, axis=1)
    tile_start = jnp.arange(rows // tm, dtype=jnp.int32) * tm
    tile_expert = jnp.minimum(jnp.searchsorted(ends, tile_start, side="right"), ne - 1).astype(jnp.int32)
    meta = jnp.concatenate([tile_expert, (ends[-1:] // tm).astype(jnp.int32)])
    return pos.reshape(m, TOP_K), meta, rows


def _wait_rows(src_ref, dst_ref, sem, n):
    pltpu.make_async_copy(src_ref.at[pl.ds(0, n)], dst_ref.at[pl.ds(0, n)], sem).wait()


def _scatter_rows_kernel(pos_ref, h_ref, init_ref, o_ref, sem):
    del init_ref
    tm = h_ref.shape[0]

    def issue(r, carry):
        for k in range(TOP_K):
            pltpu.make_async_copy(h_ref.at[pl.ds(r, 1)], o_ref.at[pl.ds(pos_ref[0, k, r], 1)], sem).start()
        return carry

    lax.fori_loop(0, tm, issue, 0, unroll=8)
    for k in range(TOP_K):
        _wait_rows(h_ref, o_ref, sem, tm)


def scatter_rows(h, pos, rows, tm):
    m, d = h.shape
    pos_t = pos.reshape(m // tm, tm, TOP_K).transpose(0, 2, 1)
    return pl.pallas_call(
        _scatter_rows_kernel,
        grid=(m // tm,),
        in_specs=[
            pl.BlockSpec((1, TOP_K, tm), lambda i: (i, 0, 0), memory_space=pltpu.SMEM),
            pl.BlockSpec((tm, d), lambda i: (i, 0)),
            pl.BlockSpec(memory_space=pl.ANY),
        ],
        out_specs=pl.BlockSpec(memory_space=pl.ANY),
        out_shape=jax.ShapeDtypeStruct((rows, d), h.dtype),
        scratch_shapes=[pltpu.SemaphoreType.DMA(())],
        input_output_aliases={2: 0},
        compiler_params=_params("arbitrary"),
        name="scatter_rows",
    )(pos_t, h, jnp.zeros((rows, d), h.dtype))


def _expert_ffn_kernel(meta_ref, h_ref, w1_ref, w3_ref, w2_ref, o_ref, hb_ref):
    i, f = pl.program_id(0), pl.program_id(1)
    used = i < meta_ref[meta_ref.shape[0] - 1]

    @pl.when(f == 0)
    def _():
        hb_ref[...] = h_ref[...].astype(BF16)
        o_ref[...] = jnp.zeros(o_ref.shape, o_ref.dtype)

    @pl.when(used)
    def _():
        o_ref[...] += _swiglu_out(hb_ref, w1_ref, w3_ref, w2_ref)


def expert_ffn(hs, w1, w3, w2, meta, tm):
    r, d = hs.shape
    fdim = w1.shape[2]
    tf = _tile(fdim, 256)
    grid_spec = pltpu.PrefetchScalarGridSpec(
        num_scalar_prefetch=1,
        grid=(r // tm, fdim // tf),
        in_specs=[
            pl.BlockSpec((tm, d), lambda i, f, meta: (i, 0)),
            pl.BlockSpec((1, d, tf), lambda i, f, meta: (meta[i], 0, f)),
            pl.BlockSpec((1, d, tf), lambda i, f, meta: (meta[i], 0, f)),
            pl.BlockSpec((1, tf, d), lambda i, f, meta: (meta[i], f, 0)),
        ],
        out_specs=pl.BlockSpec((tm, d), lambda i, f, meta: (i, 0)),
        scratch_shapes=[pltpu.VMEM((tm, d), BF16)],
    )
    return pl.pallas_call(
        _expert_ffn_kernel,
        grid_spec=grid_spec,
        out_shape=jax.ShapeDtypeStruct((r, d), F32),
        compiler_params=_params("parallel", "arbitrary"),
        name="expert_ffn",
    )(meta, hs, w1, w3, w2)


def _combine_kernel(pos_ref, ys_ref, w_ref, x_ref, g_ref, o_ref, buf_ref, sem):
    tm = x_ref.shape[0]

    def issue(r, carry):
        for k in range(TOP_K):
            pltpu.make_async_copy(ys_ref.at[pl.ds(pos_ref[0, k, r], 1)], buf_ref.at[pl.ds(k * tm + r, 1)], sem).start()
        return carry

    lax.fori_loop(0, tm, issue, 0, unroll=8)
    _wait_rows(ys_ref, buf_ref, sem, TOP_K * tm)
    w = w_ref[...]
    y = w[:, 0:1] * buf_ref[pl.ds(0, tm), :]
    for k in range(1, TOP_K):
        y = y + w[:, k:k + 1] * buf_ref[pl.ds(k * tm, tm), :]
    o_ref[...] = x_ref[...] + g_ref[0] * y


def combine_residual(ys, pos, wts, x, gate, rows_per_batch):
    m, d = x.shape
    tm = _tile(rows_per_batch, 256)
    per = rows_per_batch // tm
    pos_t = pos.reshape(m // tm, tm, TOP_K).transpose(0, 2, 1)
    return pl.pallas_call(
        _combine_kernel,
        grid=(m // tm,),
        in_specs=[
            pl.BlockSpec((1, TOP_K, tm), lambda i: (i, 0, 0), memory_space=pltpu.SMEM),
            pl.BlockSpec(memory_space=pl.ANY),
            pl.BlockSpec((tm, TOP_K), lambda i: (i, 0)),
            pl.BlockSpec((tm, d), lambda i: (i, 0)),
            pl.BlockSpec((1, 1, d), lambda i: (i // per, 0, 0)),
        ],
        out_specs=pl.BlockSpec((tm, d), lambda i: (i, 0)),
        out_shape=jax.ShapeDtypeStruct((m, d), F32),
        scratch_shapes=[pltpu.VMEM((TOP_K * tm, d), F32), pltpu.SemaphoreType.DMA(())],
        compiler_params=_params("arbitrary"),
        name="combine_residual",
    )(pos_t, ys, wts, x, gate)


def moe_residual(h, w_r, b_r, w1, w3, w2, x, gate, rows_per_batch):
    m = h.shape[0]
    tm = 1024 if m * TOP_K >= 8 * 1024 else 128
    sel, wts = router(h, w_r, b_r)
    pos, meta, rows = dispatch_plan(sel, tm)
    hs = scatter_rows(h, pos, rows, _tile(rows_per_batch, 512))
    ys = expert_ffn(hs, w1, w3, w2, meta, tm)
    return combine_residual(ys, pos, wts, x, gate, rows_per_batch)


def _pad_keys(parts, tk):
    n = sum(t.shape[1] for t in parts)
    pad = -n % tk
    if pad:
        parts = parts + [jnp.zeros((parts[0].shape[0], pad, parts[0].shape[2]), parts[0].dtype)]
    return jnp.concatenate(parts, axis=1) if len(parts) > 1 else parts[0]


def _mixer(x, n, h, proj, keys, values, lam, lam_init, tq, tk, q, p):
    b = x.shape[0]
    d = x.shape[-1]
    n_keys = sum(t.shape[1] for t in keys)
    att = diff_attention(q, keys_transposed(_pad_keys(keys, tk), tk), _pad_keys(values, tk), n_keys, lam,
                         p["subln_g"], 1 - lam_init, tq)
    sgu = spatial_gating(proj, p["sgu_ln_g"], p["sgu_ln_b"], p["w_spatial"], p["b_spatial"])
    conv = conformer_conv(proj, p["conv_w"], p["conv_b"], p["conv_ln_g"], p["conv_ln_b"])
    m = b * n
    y = merge_branches(h.reshape(m, d), att.reshape(m, -1), sgu.reshape(m, -1), conv.reshape(m, -1),
                       p["w_gate"], p["b_gate"], p["w_att_out"], p["w_sgu_out"], p["w_conv_out"])
    return proj_residual(y, p["w_o"], x.reshape(m, d), p["g1"], n).reshape(b, n, d)


def _channel(x, n, p):
    b, _, d = x.shape
    m = b * n
    x2 = x.reshape(m, d)
    if p["moe"]:
        h = normmod(x, p["norm2_g"], p["sh2"], p["sc2"], F32).reshape(m, d)
        out = moe_residual(h, p["router_w"], p["router_b"], p["w1"], p["w3"], p["w2"], x2, p["g2"], n)
    else:
        h = normmod(x, p["norm2_g"], p["sh2"], p["sc2"], BF16).reshape(m, d)
        out = ffn_residual(h, p["w1"], p["w3"], p["w2"], x2, p["g2"], n)
    return out.reshape(b, n, d)


def kernel(x, c, ctx, c_ctx, w_mod, b_mod, norm1_g, norm2_g, w_in, lam_q1, lam_k1, lam_q2, lam_k2, subln_g, w_att_out, sgu_ln_g, sgu_ln_b, w_spatial, b_spatial, w_sgu_out, conv_w, conv_b, conv_ln_g, conv_ln_b, w_conv_out, w_gate, b_gate, w_o, ffn_w1, ffn_w3, ffn_w2, router_w, router_b, moe_w1, moe_w3, moe_w2, final_g):
    b, s, d = x.shape
    nc = ctx.shape[1]
    depth = w_in.shape[0]
    cos, sin = rope_tables(s)
    ones, zeros = jnp.ones((nc, LANES), F32), jnp.zeros((nc, LANES), F32)

    c_rows = jnp.concatenate([c, c_ctx[None, :], jnp.zeros((8 - b - 1, d), F32)], axis=0)
    mods = modulation(c_rows, w_mod, b_mod)

    xc = ctx
    for i in range(depth):
        last = i == depth - 1
        lat = [t[:, None, :] for t in jnp.split(mods[i, :b], 6, axis=-1)]
        cx = [jnp.broadcast_to(t[:, None, :], (b, 1, d)) for t in jnp.split(mods[i, b:b + 1], 6, axis=-1)]
        lam_init = 0.8 - 0.6 * math.exp(-0.3 * i)
        lam = (jnp.exp(jnp.sum(lam_q1[i] * lam_k1[i]).astype(F32))
               - jnp.exp(jnp.sum(lam_q2[i] * lam_k2[i]).astype(F32)) + lam_init)
        shared = dict(
            norm2_g=norm2_g[i], subln_g=subln_g[i], sgu_ln_g=sgu_ln_g[i], sgu_ln_b=sgu_ln_b[i],
            w_spatial=w_spatial[i], b_spatial=b_spatial[i], conv_w=conv_w[i], conv_b=conv_b[i],
            conv_ln_g=conv_ln_g[i], conv_ln_b=conv_ln_b[i],
            w_gate=w_gate[i].astype(BF16), b_gate=b_gate[i], w_att_out=w_att_out[i].astype(BF16),
            w_sgu_out=w_sgu_out[i].astype(BF16), w_conv_out=w_conv_out[i].astype(BF16), w_o=w_o[i].astype(BF16),
            moe=i % 2 == 1)
        j = i // 2
        if i % 2 == 0:
            shared.update(w1=ffn_w1[j][None].astype(BF16), w3=ffn_w3[j][None].astype(BF16),
                          w2=ffn_w2[j][None].astype(BF16))
        else:
            shared.update(w1=moe_w1[j].astype(BF16), w3=moe_w3[j].astype(BF16), w2=moe_w2[j].astype(BF16),
                          router_w=router_w[j], router_b=router_b[j])
        p_lat = dict(shared, g1=lat[2], sh2=lat[3], sc2=lat[4], g2=lat[5])
        p_ctx = dict(shared, g1=cx[2], sh2=cx[3], sc2=cx[4], g2=cx[5])
        w_in_i = w_in[i].astype(BF16)

        h_lat = normmod(x, norm1_g[i], lat[0], lat[1], BF16)
        h_ctx = normmod(xc, norm1_g[i], cx[0], cx[1], BF16)
        proj_l = matmul(h_lat.reshape(b * s, d), w_in_i, BF16).reshape(b, s, IN_COLS)
        proj_c = matmul(h_ctx.reshape(b * nc, d), w_in_i, BF16).reshape(b, nc, IN_COLS)
        rope_l = rope_qk(proj_l, cos, sin)
        rope_c = rope_qk(proj_c, ones, zeros)
        v_l = proj_l[..., 2 * ATT_WIDTH:3 * ATT_WIDTH]
        v_c = proj_c[..., 2 * ATT_WIDTH:3 * ATT_WIDTH]
        x = _mixer(x, s, h_lat, proj_l, [rope_l[1], rope_c[1]], [v_l, v_c], lam, lam_init, _tile(s, 512),
                   ATT_KEY_TILE, rope_l[0], p_lat)
        if not last:
            xc = _mixer(xc, nc, h_ctx, proj_c, [rope_c[1]], [v_c], lam, lam_init, nc, nc, rope_c[0], p_ctx)

        x = _channel(x, s, p_lat)
        if not last:
            xc = _channel(xc, nc, p_ctx)

    zero = jnp.zeros((b, 1, d), F32)
    return normmod(x, final_g, zero, zero, F32)
```

```python
import functools
import math

import jax
import jax.numpy as jnp
from jax import lax
from jax.experimental import pallas as pl
from jax.experimental.pallas import tpu as pltpu

D_MODEL = 2048
BATCH = 2
SEQ = 16384
DEPTH = 4
GRID_W = 64
CTX_LEN = 256
ATT_HEADS = 8
ATT_QK_DIM = 64
ATT_V_DIM = 2 * ATT_QK_DIM
ATT_WIDTH = ATT_HEADS * ATT_V_DIM
ROPE_BASE = 10000.0
SGU_CHUNK = 128
SGU_GROUPS = 8
SGU_WIDTH = 1024
CONV_WIDTH = 1024
CONV_K = 31
IN_COLS = 3 * ATT_WIDTH + 2 * SGU_WIDTH + 2 * CONV_WIDTH
N_BRANCH = 3
FFN_DIM = 5632
N_EXPERTS = 8
TOP_K = 2
EXPERT_DIM = 2816
EPS = 1e-6

LANES = 128
SUBLANES = 8
ATT_KEY_TILE = 640
HALO = 16
VMEM_LIMIT = 56 * 1024 * 1024
BF16 = jnp.bfloat16
F32 = jnp.float32


def _params(*sem):
    return pltpu.CompilerParams(dimension_semantics=sem, vmem_limit_bytes=VMEM_LIMIT)


def _tile(n, pref):
    t = min(pref, n)
    while n % t:
        t //= 2
    return t


def _rownorm_mod(x, g, shift, scale):
    y = x * lax.rsqrt(jnp.mean(x * x, axis=-1, keepdims=True) + EPS)
    return y * g * (1 + scale) + shift


def _normmod_kernel(x_ref, g_ref, sh_ref, sc_ref, o_ref):
    o_ref[0] = _rownorm_mod(x_ref[0], g_ref[...], sh_ref[0], sc_ref[0]).astype(o_ref.dtype)


def _store_residual(x_new, o_ref, hn_ref, ng_ref, nsh_ref, nsc_ref):
    o_ref[...] = x_new
    hn_ref[...] = _rownorm_mod(x_new, ng_ref[...], nsh_ref[0], nsc_ref[0]).astype(hn_ref.dtype)


def normmod(x, g, shift, scale, out_dtype):
    b, n, d = x.shape
    tn = _tile(n, 512)
    return pl.pallas_call(
        _normmod_kernel,
        grid=(b, n // tn),
        in_specs=[
            pl.BlockSpec((1, tn, d), lambda bi, i: (bi, i, 0)),
            pl.BlockSpec((1, d), lambda bi, i: (0, 0)),
            pl.BlockSpec((1, 1, d), lambda bi, i: (bi, 0, 0)),
            pl.BlockSpec((1, 1, d), lambda bi, i: (bi, 0, 0)),
        ],
        out_specs=pl.BlockSpec((1, tn, d), lambda bi, i: (bi, i, 0)),
        out_shape=jax.ShapeDtypeStruct((b, n, d), out_dtype),
        compiler_params=_params("parallel", "parallel"),
        name="normmod",
    )(x, g.reshape(1, d), shift, scale)


def _mod_kernel(c_ref, w_ref, b_ref, o_ref):
    c = c_ref[...]
    a = (c * jax.nn.sigmoid(c)).astype(BF16)
    o_ref[0] = jnp.dot(a, w_ref[0].astype(BF16), preferred_element_type=F32) + b_ref[0]


def modulation(c_rows, w_mod, b_mod):
    r, d = c_rows.shape
    nl, _, n = w_mod.shape
    tn = _tile(n, 1024)
    return pl.pallas_call(
        _mod_kernel,
        grid=(nl, n // tn),
        in_specs=[
            pl.BlockSpec((r, d), lambda l, j: (0, 0)),
            pl.BlockSpec((1, d, tn), lambda l, j: (l, 0, j)),
            pl.BlockSpec((1, 1, tn), lambda l, j: (l, 0, j)),
        ],
        out_specs=pl.BlockSpec((1, r, tn), lambda l, j: (l, 0, j)),
        out_shape=jax.ShapeDtypeStruct((nl, r, n), F32),
        compiler_params=_params("parallel", "parallel"),
        name="modulation",
    )(c_rows, w_mod, b_mod.reshape(nl, 1, n))


def _mm_kernel(a_ref, b_ref, o_ref):
    o_ref[...] = jnp.dot(a_ref[...], b_ref[...], preferred_element_type=F32).astype(o_ref.dtype)


def matmul(a, b, out_dtype):
    m, k = a.shape
    n = b.shape[1]
    tm, tn = _tile(m, 512), _tile(n, 1024)
    return pl.pallas_call(
        _mm_kernel,
        grid=(m // tm, n // tn),
        in_specs=[
            pl.BlockSpec((tm, k), lambda i, j: (i, 0)),
            pl.BlockSpec((k, tn), lambda i, j: (0, j)),
        ],
        out_specs=pl.BlockSpec((tm, tn), lambda i, j: (i, j)),
        out_shape=jax.ShapeDtypeStruct((m, n), out_dtype),
        compiler_params=_params("parallel", "parallel"),
        name="matmul",
    )(a, b)


def _swap_halves(x):
    n = x.shape[-1]
    quarter = ATT_QK_DIM // 4
    lane = lax.broadcasted_iota(jnp.int32, x.shape, x.ndim - 1)
    up = pltpu.roll(x, n - quarter, x.ndim - 1)
    down = pltpu.roll(x, quarter, x.ndim - 1)
    return jnp.where(lane % (2 * quarter) < quarter, up, down)


def _rope_kernel(q_ref, k_ref, cos_ref, sin_ref, qo_ref, ko_ref):
    cos = cos_ref[...]
    sin = sin_ref[...]
    scale = ATT_QK_DIM ** -0.5 * math.log2(math.e)
    lane = lax.broadcasted_iota(jnp.int32, cos.shape, 1)
    first = lane < ATT_QK_DIM
    for h in range(ATT_HEADS):
        cols = slice(h * LANES, (h + 1) * LANES)
        q = q_ref[0, :, cols].astype(F32)
        k = k_ref[0, :, cols].astype(F32)
        qr = (q * cos + _swap_halves(q) * sin) * scale
        kr = k * cos + _swap_halves(k) * sin
        qo_ref[0, h, 0] = jnp.where(first, qr, 0.0).astype(qo_ref.dtype)
        qo_ref[0, h, 1] = jnp.where(first, 0.0, qr).astype(qo_ref.dtype)
        ko_ref[0, :, cols] = kr.astype(ko_ref.dtype)


def rope_qk(proj, cos, sin):
    b, n, _ = proj.shape
    tn = _tile(n, 512)
    w = ATT_WIDTH
    return pl.pallas_call(
        _rope_kernel,
        grid=(b, n // tn),
        in_specs=[
            pl.BlockSpec((1, tn, w), lambda bi, i: (bi, i, 0)),
            pl.BlockSpec((1, tn, w), lambda bi, i: (bi, i, 1)),
            pl.BlockSpec((tn, LANES), lambda bi, i: (i, 0)),
            pl.BlockSpec((tn, LANES), lambda bi, i: (i, 0)),
        ],
        out_specs=[
            pl.BlockSpec((1, ATT_HEADS, 2, tn, LANES), lambda bi, i: (bi, 0, 0, i, 0)),
            pl.BlockSpec((1, tn, w), lambda bi, i: (bi, i, 0)),
        ],
        out_shape=[
            jax.ShapeDtypeStruct((b, ATT_HEADS, 2, n, LANES), BF16),
            jax.ShapeDtypeStruct((b, n, w), BF16),
        ],
        compiler_params=_params("parallel", "parallel"),
        name="rope_qk",
    )(proj, proj, cos, sin)


def rope_tables(n):
    rows = n // GRID_W
    row = jnp.broadcast_to(jnp.arange(rows, dtype=jnp.int32)[:, None], (rows, GRID_W)).reshape(-1)
    col = jnp.broadcast_to(jnp.arange(GRID_W, dtype=jnp.int32)[None, :], (rows, GRID_W)).reshape(-1)
    half = ATT_QK_DIM // 2
    inv = ROPE_BASE ** (-jnp.arange(0, half, 2, dtype=F32) / half)
    ang_r = row.astype(F32)[:, None] * inv
    ang_c = col.astype(F32)[:, None] * inv
    cr, sr, cc, sc = jnp.cos(ang_r), jnp.sin(ang_r), jnp.cos(ang_c), jnp.sin(ang_c)
    cos = jnp.concatenate([cr, cr, cc, cc], axis=-1)
    sin = jnp.concatenate([-sr, sr, -sc, sc], axis=-1)
    reps = LANES // ATT_QK_DIM
    return jnp.tile(cos, (1, reps)), jnp.tile(sin, (1, reps))


def _attn_kernel(lam_ref, q_ref, kt_ref, v_ref, g_ref, o_ref, m_ref, acc_ref, alpha_ref, s_ref, p_ref,
                 *, nk, tk, tq, rb, unroll, last_valid, post_scale):
    rows_all = 2 * tq

    def scores(j, slot):
        q = q_ref[0, 0].reshape(rows_all, LANES)
        s_ref[slot] = jnp.dot(q, kt_ref[0, 0, j], preferred_element_type=F32)

    def weighted_values(j, slot):
        v = v_ref[0, pl.ds(pl.multiple_of(j * tk, tk), tk), :]
        vo = jnp.concatenate([v, jnp.ones((tk, LANES), BF16)], axis=1)
        alpha = alpha_ref[slot]
        acc_ref[...] = (jnp.concatenate([alpha, alpha], axis=1) * acc_ref[...]
                        + jnp.dot(p_ref[slot], vo, preferred_element_type=F32))

    def softmax(slot, valid=tk):
        for r in range(rows_all // rb):
            rows = pl.ds(r * rb, rb)
            s = s_ref[slot, rows, :]
            if valid < tk:
                s = jnp.where(lax.broadcasted_iota(jnp.int32, s.shape, 1) < valid, s, -jnp.inf)
            m_old = m_ref[rows, :]
            m_new = jnp.maximum(m_old, jnp.max(s, axis=-1, keepdims=True))
            alpha_ref[slot, rows, :] = jnp.exp2(m_old - m_new)
            p_ref[slot, rows, :] = jnp.exp2(s - m_new[:, :1]).astype(BF16)
            m_ref[rows, :] = m_new

    def stage(j, slot, first=False, last=False):
        if not last:
            scores(j + 1, 1 - slot)
        if not first:
            weighted_values(j - 1, 1 - slot)
        softmax(slot, last_valid if last else tk)

    def group(jj, carry):
        for t in range(unroll):
            stage(unroll * jj + 1 + t, (1 + t) % 2)
        return carry

    m_ref[...] = jnp.full(m_ref.shape, -jnp.inf, F32)
    acc_ref[...] = jnp.zeros(acc_ref.shape, F32)
    scores(0, 0)
    stage(0, 0, first=True, last=nk == 1)
    ngroups = (nk - 2) // unroll if nk >= 2 else 0
    lax.fori_loop(0, ngroups, group, 0)
    for j in range(1 + unroll * ngroups, nk):
        stage(j, j % 2, last=j == nk - 1)
    weighted_values(nk - 1, (nk - 1) % 2)

    acc = acc_ref[...]
    o = acc[:, :LANES] / acc[:, LANES:]
    o = o[:tq] - lam_ref[0, 0] * o[tq:]
    o = o * lax.rsqrt(jnp.mean(o * o, axis=-1, keepdims=True) + EPS)
    o_ref[0] = (o * g_ref[...] * post_scale).astype(o_ref.dtype)


def diff_attention(q, kt, v, n_valid, lam, subln_g, post_scale, tq):
    b, h, _, s, _ = q.shape
    nk, tk = kt.shape[2], kt.shape[4]
    nkeys = v.shape[1]
    rows_all = 2 * tq
    last_valid = n_valid - (nk - 1) * tk
    assert nkeys == nk * tk and 0 < last_valid <= tk
    kern = functools.partial(_attn_kernel, nk=nk, tk=tk, tq=tq, rb=_tile(rows_all, 64), unroll=2, last_valid=last_valid,
                             post_scale=post_scale)
    return pl.pallas_call(
        kern,
        grid=(b, h, s // tq),
        in_specs=[
            pl.BlockSpec(memory_space=pltpu.SMEM),
            pl.BlockSpec((1, 1, 2, tq, LANES), lambda bi, hi, i: (bi, hi, 0, i, 0)),
            pl.BlockSpec((1, 1, nk, LANES, tk), lambda bi, hi, i: (bi, hi, 0, 0, 0)),
            pl.BlockSpec((1, nkeys, LANES), lambda bi, hi, i: (bi, 0, hi)),
            pl.BlockSpec((1, LANES), lambda bi, hi, i: (0, 0)),
        ],
        out_specs=pl.BlockSpec((1, tq, LANES), lambda bi, hi, i: (bi, i, hi)),
        out_shape=jax.ShapeDtypeStruct((b, s, h * LANES), BF16),
        scratch_shapes=[
            pltpu.VMEM((rows_all, LANES), F32),
            pltpu.VMEM((rows_all, 2 * LANES), F32),
            pltpu.VMEM((2, rows_all, LANES), F32),
            pltpu.VMEM((2, rows_all, tk), F32),
            pltpu.VMEM((2, rows_all, tk), BF16),
        ],
        compiler_params=_params("parallel", "parallel", "parallel"),
        name="diff_attention",
    )(lam.reshape(1, 1), q, kt, v, subln_g.reshape(1, LANES))


def keys_transposed(k, tk):
    b, nkeys, _ = k.shape
    k = k.reshape(b, nkeys // tk, tk, ATT_HEADS, LANES)
    return k.transpose(0, 3, 1, 4, 2)


def _sgu_kernel(u_ref, v_ref, g_ref, b_ref, ws_ref, bs_ref, o_ref):
    v = jax.nn.gelu(v_ref[0].astype(F32))
    mu = jnp.mean(v, axis=-1, keepdims=True)
    var = jnp.mean(jnp.square(v - mu), axis=-1, keepdims=True)
    v = ((v - mu) * lax.rsqrt(var + EPS) * g_ref[...] + b_ref[...]).astype(BF16)
    tn = v.shape[0]
    cw = SGU_WIDTH // SGU_GROUPS
    for c in range(tn // SGU_CHUNK):
        rows = slice(c * SGU_CHUNK, (c + 1) * SGU_CHUNK)
        for g in range(SGU_GROUPS):
            cols = slice(g * cw, (g + 1) * cw)
            mixed = jnp.dot(ws_ref[g], v[rows, cols], preferred_element_type=F32) + bs_ref[g]
            u = jax.nn.gelu(u_ref[0, rows, cols].astype(F32))
            o_ref[0, rows, cols] = (u * mixed).astype(o_ref.dtype)


def spatial_gating(proj, ln_g, ln_b, w_s, b_s):
    b, n, _ = proj.shape
    tn = _tile(n, 512)
    w = SGU_WIDTH
    u_blk = 3 * ATT_WIDTH // w
    return pl.pallas_call(
        _sgu_kernel,
        grid=(b, n // tn),
        in_specs=[
            pl.BlockSpec((1, tn, w), lambda bi, i: (bi, i, u_blk)),
            pl.BlockSpec((1, tn, w), lambda bi, i: (bi, i, u_blk + 1)),
            pl.BlockSpec((1, w), lambda bi, i: (0, 0)),
            pl.BlockSpec((1, w), lambda bi, i: (0, 0)),
            pl.BlockSpec((SGU_GROUPS, SGU_CHUNK, SGU_CHUNK), lambda bi, i: (0, 0, 0)),
            pl.BlockSpec((SGU_GROUPS, SGU_CHUNK, 1), lambda bi, i: (0, 0, 0)),
        ],
        out_specs=pl.BlockSpec((1, tn, w), lambda bi, i: (bi, i, 0)),
        out_shape=jax.ShapeDtypeStruct((b, n, w), BF16),
        compiler_params=_params("parallel", "parallel"),
        name="spatial_gating",
    )(proj, proj, ln_g.reshape(1, w), ln_b.reshape(1, w), w_s.astype(BF16),
      b_s.reshape(SGU_GROUPS, SGU_CHUNK, 1))


def _glu(a_ref, g_ref):
    return a_ref[0].astype(F32) * jax.nn.sigmoid(g_ref[0].astype(F32))


def _conv_kernel(a_ref, g_ref, ap_ref, gp_ref, an_ref, gn_ref, w_ref, b_ref, lg_ref, lb_ref, o_ref, h_ref, *, rc):
    i = pl.program_id(1)
    tn = a_ref.shape[1]
    h_ref[pl.ds(HALO, tn), :] = _glu(a_ref, g_ref)
    h_ref[pl.ds(0, HALO), :] = jnp.where(i > 0, _glu(ap_ref, gp_ref), 0.0)
    h_ref[pl.ds(HALO + tn, HALO), :] = jnp.where(i < pl.num_programs(1) - 1, _glu(an_ref, gn_ref), 0.0)
    first = HALO - CONV_K // 2
    wrows = rc + 2 * HALO
    sub = 8

    def chunk(r, carry):
        r0 = pl.multiple_of(r * rc, rc)
        cols = []
        for c in range(CONV_WIDTH // LANES):
            lanes = pl.ds(c * LANES, LANES)
            win = h_ref[pl.ds(r0, wrows), lanes]
            acc = jnp.zeros((rc, LANES), F32) + b_ref[:, lanes]
            for rot in range(sub):
                shifted = win if rot == 0 else pltpu.roll(win, wrows - rot, 0)
                for k in range(CONV_K):
                    off = first + k
                    if off % sub == rot:
                        base = off - rot
                        acc = acc + w_ref[pl.ds(k, 1), lanes] * shifted[base:base + rc]
            cols.append(acc)
        acc = jnp.concatenate(cols, axis=1)
        mu = jnp.mean(acc, axis=-1, keepdims=True)
        var = jnp.mean(jnp.square(acc - mu), axis=-1, keepdims=True)
        y = (acc - mu) * lax.rsqrt(var + EPS) * lg_ref[...] + lb_ref[...]
        o_ref[0, pl.ds(r0, rc), :] = (y * jax.nn.sigmoid(y)).astype(o_ref.dtype)
        return carry

    lax.fori_loop(0, tn // rc, chunk, 0)


def conformer_conv(proj, w_dw, b_dw, ln_g, ln_b):
    b, n, _ = proj.shape
    tn = _tile(n, 512)
    w = CONV_WIDTH
    a_blk = (3 * ATT_WIDTH + 2 * SGU_WIDTH) // w
    nh = n // HALO
    per = tn // HALO

    def main(c):
        return pl.BlockSpec((1, tn, w), lambda bi, i: (bi, i, c))

    def prev(c):
        return pl.BlockSpec((1, HALO, w), lambda bi, i: (bi, jnp.maximum(i * per - 1, 0), c))

    def nxt(c):
        return pl.BlockSpec((1, HALO, w), lambda bi, i: (bi, jnp.minimum((i + 1) * per, nh - 1), c))

    vec = pl.BlockSpec((1, w), lambda bi, i: (0, 0))
    return pl.pallas_call(
        functools.partial(_conv_kernel, rc=_tile(tn, 64)),
        grid=(b, n // tn),
        in_specs=[main(a_blk), main(a_blk + 1), prev(a_blk), prev(a_blk + 1), nxt(a_blk), nxt(a_blk + 1),
                  pl.BlockSpec((CONV_K, w), lambda bi, i: (0, 0)), vec, vec, vec],
        out_specs=pl.BlockSpec((1, tn, w), lambda bi, i: (bi, i, 0)),
        out_shape=jax.ShapeDtypeStruct((b, n, w), BF16),
        scratch_shapes=[pltpu.VMEM((tn + 2 * HALO, w), F32)],
        compiler_params=_params("parallel", "parallel"),
        name="conformer_conv",
    )(proj, proj, proj, proj, proj, proj, w_dw, b_dw.reshape(1, w), ln_g.reshape(1, w), ln_b.reshape(1, w))


def _merge_kernel(h_ref, att_ref, sgu_ref, conv_ref, wg0, wg1, wg2, bg0, bg1, bg2, wa, ws, wc, o_ref):
    h = h_ref[...]
    y = None
    for br_ref, wg, bg, wo in ((att_ref, wg0, bg0, wa), (sgu_ref, wg1, bg1, ws), (conv_ref, wg2, bg2, wc)):
        gate = jax.nn.sigmoid(jnp.dot(h, wg[...], preferred_element_type=F32) + bg[...])
        t = gate * jnp.dot(br_ref[...], wo[...], preferred_element_type=F32)
        y = t if y is None else y + t
    o_ref[...] = y.astype(o_ref.dtype)


def merge_branches(h, att, sgu, conv, w_gate, b_gate, w_att_out, w_sgu_out, w_conv_out):
    m, d = h.shape
    tm, tn = _tile(m, 512), _tile(d, 512)
    nj = d // tn
    bw = att.shape[1]

    def wg(br):
        return pl.BlockSpec((d, tn), lambda i, j: (0, br * nj + j))

    def bg(br):
        return pl.BlockSpec((1, tn), lambda i, j: (0, br * nj + j))

    row = pl.BlockSpec((tm, bw), lambda i, j: (i, 0))
    wout = pl.BlockSpec((bw, tn), lambda i, j: (0, j))
    b_gate = b_gate.reshape(1, N_BRANCH * d)
    return pl.pallas_call(
        _merge_kernel,
        grid=(m // tm, nj),
        in_specs=[pl.BlockSpec((tm, d), lambda i, j: (i, 0)), row, row, row,
                  wg(0), wg(1), wg(2), bg(0), bg(1), bg(2), wout, wout, wout],
        out_specs=pl.BlockSpec((tm, tn), lambda i, j: (i, j)),
        out_shape=jax.ShapeDtypeStruct((m, d), BF16),
        compiler_params=_params("parallel", "parallel"),
        name="merge_branches",
    )(h, att, sgu, conv, w_gate, w_gate, w_gate, b_gate, b_gate, b_gate, w_att_out, w_sgu_out, w_conv_out)


def _proj_res_kernel(y_ref, w_ref, x_ref, g_ref, ng_ref, nsh_ref, nsc_ref, o_ref, hn_ref):
    x_new = x_ref[...] + g_ref[0] * jnp.dot(y_ref[...], w_ref[...], preferred_element_type=F32)
    _store_residual(x_new, o_ref, hn_ref, ng_ref, nsh_ref, nsc_ref)


def _next_norm_args(norm, d):
    g, shift, scale = norm
    return g.reshape(1, d), shift, scale


def proj_residual(y, w, x, gate, rows_per_batch, norm, h_dtype):
    m, k = y.shape
    d = w.shape[1]
    tm = _tile(rows_per_batch, 512)
    per = rows_per_batch // tm
    row = pl.BlockSpec((tm, d), lambda i: (i, 0))
    per_batch = pl.BlockSpec((1, 1, d), lambda i: (i // per, 0, 0))
    return pl.pallas_call(
        _proj_res_kernel,
        grid=(m // tm,),
        in_specs=[
            pl.BlockSpec((tm, k), lambda i: (i, 0)),
            pl.BlockSpec((k, d), lambda i: (0, 0)),
            row, per_batch,
            pl.BlockSpec((1, d), lambda i: (0, 0)), per_batch, per_batch,
        ],
        out_specs=[row, row],
        out_shape=[jax.ShapeDtypeStruct((m, d), F32), jax.ShapeDtypeStruct((m, d), h_dtype)],
        compiler_params=_params("parallel"),
        name="proj_residual",
    )(y, w, x, gate, *_next_norm_args(norm, d))


def _swiglu_out(h_ref, w1_ref, w3_ref, w2_ref, parts=1):
    rows = h_ref.shape[0] // parts
    outs = []
    for r in range(parts):
        h = h_ref[pl.ds(r * rows, rows), :]
        a = jnp.dot(h, w1_ref[0], preferred_element_type=F32)
        b = jnp.dot(h, w3_ref[0], preferred_element_type=F32)
        z = (a * jax.nn.sigmoid(a) * b).astype(BF16)
        outs.append(jnp.dot(z, w2_ref[0], preferred_element_type=F32))
    return jnp.concatenate(outs, axis=0)


def _ffn_kernel(h_ref, w1_ref, w3_ref, w2_ref, x_ref, g_ref, ng_ref, nsh_ref, nsc_ref, o_ref, hn_ref, acc_ref):
    f = pl.program_id(1)
    y = _swiglu_out(h_ref, w1_ref, w3_ref, w2_ref)

    @pl.when(f == 0)
    def _():
        acc_ref[...] = y

    @pl.when(f > 0)
    def _():
        acc_ref[...] += y

    @pl.when(f == pl.num_programs(1) - 1)
    def _():
        _store_residual(x_ref[...] + g_ref[0] * acc_ref[...], o_ref, hn_ref, ng_ref, nsh_ref, nsc_ref)


def ffn_residual(h, w1, w3, w2, x, gate, rows_per_batch, norm, h_dtype):
    m, d = h.shape
    fdim = w1.shape[2]
    tm = _tile(rows_per_batch, 512)
    tf = _tile(fdim, 512)
    per = rows_per_batch // tm
    row = pl.BlockSpec((tm, d), lambda i, f: (i, 0))
    per_batch = pl.BlockSpec((1, 1, d), lambda i, f: (i // per, 0, 0))
    return pl.pallas_call(
        _ffn_kernel,
        grid=(m // tm, fdim // tf),
        in_specs=[
            row,
            pl.BlockSpec((1, d, tf), lambda i, f: (0, 0, f)),
            pl.BlockSpec((1, d, tf), lambda i, f: (0, 0, f)),
            pl.BlockSpec((1, tf, d), lambda i, f: (0, f, 0)),
            row, per_batch,
            pl.BlockSpec((1, d), lambda i, f: (0, 0)), per_batch, per_batch,
        ],
        out_specs=[row, row],
        out_shape=[jax.ShapeDtypeStruct((m, d), F32), jax.ShapeDtypeStruct((m, d), h_dtype)],
        scratch_shapes=[pltpu.VMEM((tm, d), F32)],
        compiler_params=_params("parallel", "arbitrary"),
        name="ffn_residual",
    )(h, w1, w3, w2, x, gate, *_next_norm_args(norm, d))


def _router_kernel(h_ref, w_ref, b_ref, sel_ref, wts_ref):
    logits = jnp.dot(h_ref[...].astype(BF16), w_ref[...], preferred_element_type=F32) + b_ref[...]
    ne = logits.shape[-1]
    idx = lax.broadcasted_iota(jnp.int32, logits.shape, 1)
    v1 = jnp.max(logits, axis=-1, keepdims=True)
    i1 = jnp.min(jnp.where(logits == v1, idx, ne), axis=-1, keepdims=True)
    rest = jnp.where(idx == i1, -jnp.inf, logits)
    v2 = jnp.max(rest, axis=-1, keepdims=True)
    i2 = jnp.min(jnp.where(rest == v2, idx, ne), axis=-1, keepdims=True)
    e2 = jnp.exp(v2 - v1)
    sel_ref[...] = jnp.concatenate([i1, i2], axis=1)
    wts_ref[...] = jnp.concatenate([1.0 / (1.0 + e2), e2 / (1.0 + e2)], axis=1)


def router(h, w_r, b_r):
    m, d = h.shape
    ne = w_r.shape[1]
    tm = _tile(m, 512)
    return pl.pallas_call(
        _router_kernel,
        grid=(m // tm,),
        in_specs=[
            pl.BlockSpec((tm, d), lambda i: (i, 0)),
            pl.BlockSpec((d, ne), lambda i: (0, 0)),
            pl.BlockSpec((1, ne), lambda i: (0, 0)),
        ],
        out_specs=[pl.BlockSpec((tm, TOP_K), lambda i: (i, 0)), pl.BlockSpec((tm, TOP_K), lambda i: (i, 0))],
        out_shape=[jax.ShapeDtypeStruct((m, TOP_K), jnp.int32), jax.ShapeDtypeStruct((m, TOP_K), F32)],
        compiler_params=_params("parallel"),
        name="router",
    )(h, w_r.astype(BF16), b_r.reshape(1, ne))


def dispatch_plan(sel, tm):
    m = sel.shape[0]
    ne = N_EXPERTS
    npairs = m * TOP_K
    rows = npairs + ne * tm
    e_flat = sel.reshape(npairs)
    onehot = (e_flat[:, None] == jnp.arange(ne, dtype=jnp.int32)[None, :]).astype(jnp.int32)
    rank = jnp.cumsum(onehot, axis=0) - onehot
    counts = jnp.sum(onehot, axis=0)
    padded = (counts + tm - 1) // tm * tm
    ends = jnp.cumsum(padded)
    pos = (ends - padded)[e_flat] + jnp.sum(rank * onehot, axis=1)
    tile_start = jnp.arange(rows // tm, dtype=jnp.int32) * tm
    tile_expert = jnp.minimum(jnp.searchsorted(ends, tile_start, side="right"), ne - 1).astype(jnp.int32)
    meta = jnp.concatenate([tile_expert, (ends[-1:] // tm).astype(jnp.int32)])
    return pos.reshape(m, TOP_K), meta, rows


def _wait_rows(src_ref, dst_ref, sem, n):
    pltpu.make_async_copy(src_ref.at[pl.ds(0, n)], dst_ref.at[pl.ds(0, n)], sem).wait()


def _scatter_rows_kernel(pos_ref, h_ref, init_ref, o_ref, sem):
    del init_ref
    tm = h_ref.shape[0]

    def issue(r, carry):
        for k in range(TOP_K):
            pltpu.make_async_copy(h_ref.at[pl.ds(r, 1)], o_ref.at[pl.ds(pos_ref[0, k, r], 1)], sem).start()
        return carry

    lax.fori_loop(0, tm, issue, 0, unroll=8)
    for k in range(TOP_K):
        _wait_rows(h_ref, o_ref, sem, tm)


def scatter_rows(h, pos, rows, tm):
    m, d = h.shape
    pos_t = pos.reshape(m // tm, tm, TOP_K).transpose(0, 2, 1)
    return pl.pallas_call(
        _scatter_rows_kernel,
        grid=(m // tm,),
        in_specs=[
            pl.BlockSpec((1, TOP_K, tm), lambda i: (i, 0, 0), memory_space=pltpu.SMEM),
            pl.BlockSpec((tm, d), lambda i: (i, 0)),
            pl.BlockSpec(memory_space=pl.ANY),
        ],
        out_specs=pl.BlockSpec(memory_space=pl.ANY),
        out_shape=jax.ShapeDtypeStruct((rows, d), h.dtype),
        scratch_shapes=[pltpu.SemaphoreType.DMA(())],
        input_output_aliases={2: 0},
        compiler_params=_params("arbitrary"),
        name="scatter_rows",
    )(pos_t, h, jnp.zeros((rows, d), h.dtype))


def _expert_ffn_kernel(meta_ref, h_ref, w1_ref, w3_ref, w2_ref, o_ref, hb_ref):
    i, f = pl.program_id(0), pl.program_id(1)
    used = i < meta_ref[meta_ref.shape[0] - 1]

    @pl.when(f == 0)
    def _():
        hb_ref[...] = h_ref[...].astype(BF16)
        o_ref[...] = jnp.zeros(o_ref.shape, o_ref.dtype)

    @pl.when(used)
    def _():
        o_ref[...] += _swiglu_out(hb_ref, w1_ref, w3_ref, w2_ref)


def expert_ffn(hs, w1, w3, w2, meta, tm):
    r, d = hs.shape
    fdim = w1.shape[2]
    tf = _tile(fdim, 256)
    grid_spec = pltpu.PrefetchScalarGridSpec(
        num_scalar_prefetch=1,
        grid=(r // tm, fdim // tf),
        in_specs=[
            pl.BlockSpec((tm, d), lambda i, f, meta: (i, 0)),
            pl.BlockSpec((1, d, tf), lambda i, f, meta: (meta[i], 0, f)),
            pl.BlockSpec((1, d, tf), lambda i, f, meta: (meta[i], 0, f)),
            pl.BlockSpec((1, tf, d), lambda i, f, meta: (meta[i], f, 0)),
        ],
        out_specs=pl.BlockSpec((tm, d), lambda i, f, meta: (i, 0)),
        scratch_shapes=[pltpu.VMEM((tm, d), BF16)],
    )
    return pl.pallas_call(
        _expert_ffn_kernel,
        grid_spec=grid_spec,
        out_shape=jax.ShapeDtypeStruct((r, d), F32),
        compiler_params=_params("parallel", "arbitrary"),
        name="expert_ffn",
    )(meta, hs, w1, w3, w2)


def _combine_kernel(pos_ref, ys_ref, w_ref, x_ref, g_ref, ng_ref, nsh_ref, nsc_ref, o_ref, hn_ref, buf_ref, sem):
    tm = x_ref.shape[0]

    def issue(r, carry):
        for k in range(TOP_K):
            pltpu.make_async_copy(ys_ref.at[pl.ds(pos_ref[0, k, r], 1)], buf_ref.at[pl.ds(k * tm + r, 1)], sem).start()
        return carry

    lax.fori_loop(0, tm, issue, 0, unroll=8)
    _wait_rows(ys_ref, buf_ref, sem, TOP_K * tm)
    w = w_ref[...]
    y = w[:, 0:1] * buf_ref[pl.ds(0, tm), :]
    for k in range(1, TOP_K):
        y = y + w[:, k:k + 1] * buf_ref[pl.ds(k * tm, tm), :]
    _store_residual(x_ref[...] + g_ref[0] * y, o_ref, hn_ref, ng_ref, nsh_ref, nsc_ref)


def combine_residual(ys, pos, wts, x, gate, rows_per_batch, norm, h_dtype):
    m, d = x.shape
    tm = _tile(rows_per_batch, 256)
    per = rows_per_batch // tm
    pos_t = pos.reshape(m // tm, tm, TOP_K).transpose(0, 2, 1)
    row = pl.BlockSpec((tm, d), lambda i: (i, 0))
    per_batch = pl.BlockSpec((1, 1, d), lambda i: (i // per, 0, 0))
    return pl.pallas_call(
        _combine_kernel,
        grid=(m // tm,),
        in_specs=[
            pl.BlockSpec((1, TOP_K, tm), lambda i: (i, 0, 0), memory_space=pltpu.SMEM),
            pl.BlockSpec(memory_space=pl.ANY),
            pl.BlockSpec((tm, TOP_K), lambda i: (i, 0)),
            row, per_batch,
            pl.BlockSpec((1, d), lambda i: (0, 0)), per_batch, per_batch,
        ],
        out_specs=[row, row],
        out_shape=[jax.ShapeDtypeStruct((m, d), F32), jax.ShapeDtypeStruct((m, d), h_dtype)],
        scratch_shapes=[pltpu.VMEM((TOP_K * tm, d), F32), pltpu.SemaphoreType.DMA(())],
        compiler_params=_params("arbitrary"),
        name="combine_residual",
    )(pos_t, ys, wts, x, gate, *_next_norm_args(norm, d))


def moe_residual(h, w_r, b_r, w1, w3, w2, x, gate, rows_per_batch, norm, h_dtype):
    m = h.shape[0]
    tm = 1024 if m * TOP_K >= 8 * 1024 else 128
    sel, wts = router(h, w_r, b_r)
    pos, meta, rows = dispatch_plan(sel, tm)
    hs = scatter_rows(h, pos, rows, _tile(rows_per_batch, 512))
    ys = expert_ffn(hs, w1, w3, w2, meta, tm)
    return combine_residual(ys, pos, wts, x, gate, rows_per_batch, norm, h_dtype)


def _pad_keys(parts, tk):
    n = sum(t.shape[1] for t in parts)
    pad = -n % tk
    if pad:
        parts = parts + [jnp.zeros((parts[0].shape[0], pad, parts[0].shape[2]), parts[0].dtype)]
    return jnp.concatenate(parts, axis=1) if len(parts) > 1 else parts[0]


def _mixer(x, n, h, proj, keys, values, lam, lam_init, tq, tk, q, p):
    m, d = x.shape
    n_keys = sum(t.shape[1] for t in keys)
    att = diff_attention(q, keys_transposed(_pad_keys(keys, tk), tk), _pad_keys(values, tk), n_keys, lam,
                         p["subln_g"], 1 - lam_init, tq)
    sgu = spatial_gating(proj, p["sgu_ln_g"], p["sgu_ln_b"], p["w_spatial"], p["b_spatial"])
    conv = conformer_conv(proj, p["conv_w"], p["conv_b"], p["conv_ln_g"], p["conv_ln_b"])
    y = merge_branches(h, att.reshape(m, -1), sgu.reshape(m, -1), conv.reshape(m, -1),
                       p["w_gate"], p["b_gate"], p["w_att_out"], p["w_sgu_out"], p["w_conv_out"])
    norm2 = (p["norm2_g"], p["sh2"], p["sc2"])
    return proj_residual(y, p["w_o"], x, p["g1"], n, norm2, F32 if p["moe"] else BF16)


def _channel(x, h, n, p, next_norm, next_dtype):
    if p["moe"]:
        return moe_residual(h, p["router_w"], p["router_b"], p["w1"], p["w3"], p["w2"], x, p["g2"], n,
                            next_norm, next_dtype)
    return ffn_residual(h, p["w1"], p["w3"], p["w2"], x, p["g2"], n, next_norm, next_dtype)


def kernel(x, c, ctx, c_ctx, w_mod, b_mod, norm1_g, norm2_g, w_in, lam_q1, lam_k1, lam_q2, lam_k2, subln_g, w_att_out, sgu_ln_g, sgu_ln_b, w_spatial, b_spatial, w_sgu_out, conv_w, conv_b, conv_ln_g, conv_ln_b, w_conv_out, w_gate, b_gate, w_o, ffn_w1, ffn_w3, ffn_w2, router_w, router_b, moe_w1, moe_w3, moe_w2, final_g):
    b, s, d = x.shape
    nc = ctx.shape[1]
    depth = w_in.shape[0]
    cos, sin = rope_tables(s)
    ones, zeros = jnp.ones((nc, LANES), F32), jnp.zeros((nc, LANES), F32)

    c_rows = jnp.concatenate([c, c_ctx[None, :], jnp.zeros((SUBLANES - b - 1, d), F32)], axis=0)
    mods = modulation(c_rows, w_mod, b_mod)
    lats = [[t[:, None, :] for t in jnp.split(mods[i, :b], 6, axis=-1)] for i in range(depth)]
    cxs = [[jnp.broadcast_to(t[:, None, :], (b, 1, d)) for t in jnp.split(mods[i, b:b + 1], 6, axis=-1)]
           for i in range(depth)]
    zero = jnp.zeros((b, 1, d), F32)

    x = x.reshape(b * s, d)
    xc = ctx.reshape(b * nc, d)
    h_lat = normmod(x.reshape(b, s, d), norm1_g[0], lats[0][0], lats[0][1], BF16).reshape(b * s, d)
    h_ctx = normmod(ctx, norm1_g[0], cxs[0][0], cxs[0][1], BF16).reshape(b * nc, d)
    for i in range(depth):
        last = i == depth - 1
        lat, cx = lats[i], cxs[i]
        lam_init = 0.8 - 0.6 * math.exp(-0.3 * i)
        lam = (jnp.exp(jnp.sum(lam_q1[i] * lam_k1[i]).astype(F32))
               - jnp.exp(jnp.sum(lam_q2[i] * lam_k2[i]).astype(F32)) + lam_init)
        shared = dict(
            norm2_g=norm2_g[i], subln_g=subln_g[i], sgu_ln_g=sgu_ln_g[i], sgu_ln_b=sgu_ln_b[i],
            w_spatial=w_spatial[i], b_spatial=b_spatial[i], conv_w=conv_w[i], conv_b=conv_b[i],
            conv_ln_g=conv_ln_g[i], conv_ln_b=conv_ln_b[i],
            w_gate=w_gate[i].astype(BF16), b_gate=b_gate[i], w_att_out=w_att_out[i].astype(BF16),
            w_sgu_out=w_sgu_out[i].astype(BF16), w_conv_out=w_conv_out[i].astype(BF16), w_o=w_o[i].astype(BF16),
            moe=i % 2 == 1)
        j = i // 2
        if i % 2 == 0:
            shared.update(w1=ffn_w1[j][None].astype(BF16), w3=ffn_w3[j][None].astype(BF16),
                          w2=ffn_w2[j][None].astype(BF16))
        else:
            shared.update(w1=moe_w1[j].astype(BF16), w3=moe_w3[j].astype(BF16), w2=moe_w2[j].astype(BF16),
                          router_w=router_w[j], router_b=router_b[j])
        p_lat = dict(shared, g1=lat[2], sh2=lat[3], sc2=lat[4], g2=lat[5])
        p_ctx = dict(shared, g1=cx[2], sh2=cx[3], sc2=cx[4], g2=cx[5])
        w_in_i = w_in[i].astype(BF16)

        proj_l = matmul(h_lat, w_in_i, BF16).reshape(b, s, IN_COLS)
        proj_c = matmul(h_ctx, w_in_i, BF16).reshape(b, nc, IN_COLS)
        rope_l = rope_qk(proj_l, cos, sin)
        rope_c = rope_qk(proj_c, ones, zeros)
        v_l = proj_l[..., 2 * ATT_WIDTH:3 * ATT_WIDTH]
        v_c = proj_c[..., 2 * ATT_WIDTH:3 * ATT_WIDTH]
        x, h2_lat = _mixer(x, s, h_lat, proj_l, [rope_l[1], rope_c[1]], [v_l, v_c], lam, lam_init,
                           _tile(s, 512), ATT_KEY_TILE, rope_l[0], p_lat)
        if not last:
            xc, h2_ctx = _mixer(xc, nc, h_ctx, proj_c, [rope_c[1]], [v_c], lam, lam_init, nc, nc, rope_c[0], p_ctx)

        if last:
            _, out = _channel(x, h2_lat, s, p_lat, (final_g, zero, zero), F32)
            return out.reshape(b, s, d)
        x, h_lat = _channel(x, h2_lat, s, p_lat, (norm1_g[i + 1], lats[i + 1][0], lats[i + 1][1]), BF16)
        xc, h_ctx = _channel(xc, h2_ctx, nc, p_ctx, (norm1_g[i + 1], cxs[i + 1][0], cxs[i + 1][1]), BF16)
```

```python
import functools
import math

import jax
import jax.numpy as jnp
from jax import lax
from jax.experimental import pallas as pl
from jax.experimental.pallas import tpu as pltpu

D_MODEL = 2048
BATCH = 2
SEQ = 16384
DEPTH = 4
GRID_W = 64
CTX_LEN = 256
ATT_HEADS = 8
ATT_QK_DIM = 64
ATT_V_DIM = 2 * ATT_QK_DIM
ATT_WIDTH = ATT_HEADS * ATT_V_DIM
ROPE_BASE = 10000.0
SGU_CHUNK = 128
SGU_GROUPS = 8
SGU_WIDTH = 1024
CONV_WIDTH = 1024
CONV_K = 31
IN_COLS = 3 * ATT_WIDTH + 2 * SGU_WIDTH + 2 * CONV_WIDTH
N_BRANCH = 3
FFN_DIM = 5632
N_EXPERTS = 8
TOP_K = 2
EXPERT_DIM = 2816
EPS = 1e-6

LANES = 128
SUBLANES = 8
ATT_KEY_TILE = 640
HALO = 16
VMEM_LIMIT = 56 * 1024 * 1024
BF16 = jnp.bfloat16
F32 = jnp.float32


def _params(*sem):
    return pltpu.CompilerParams(dimension_semantics=sem, vmem_limit_bytes=VMEM_LIMIT)


def _tile(n, pref):
    t = min(pref, n)
    while n % t:
        t //= 2
    return t


def _rownorm_mod(x, g, shift, scale):
    y = x * lax.rsqrt(jnp.mean(x * x, axis=-1, keepdims=True) + EPS)
    return y * g * (1 + scale) + shift


def _normmod_kernel(x_ref, g_ref, sh_ref, sc_ref, o_ref):
    o_ref[0] = _rownorm_mod(x_ref[0], g_ref[...], sh_ref[0], sc_ref[0]).astype(o_ref.dtype)


def _store_residual(x_new, o_ref, hn_ref, ng_ref, nsh_ref, nsc_ref):
    o_ref[...] = x_new
    hn_ref[...] = _rownorm_mod(x_new, ng_ref[...], nsh_ref[0], nsc_ref[0]).astype(hn_ref.dtype)


def normmod(x, g, shift, scale, out_dtype):
    b, n, d = x.shape
    tn = _tile(n, 512)
    return pl.pallas_call(
        _normmod_kernel,
        grid=(b, n // tn),
        in_specs=[
            pl.BlockSpec((1, tn, d), lambda bi, i: (bi, i, 0)),
            pl.BlockSpec((1, d), lambda bi, i: (0, 0)),
            pl.BlockSpec((1, 1, d), lambda bi, i: (bi, 0, 0)),
            pl.BlockSpec((1, 1, d), lambda bi, i: (bi, 0, 0)),
        ],
        out_specs=pl.BlockSpec((1, tn, d), lambda bi, i: (bi, i, 0)),
        out_shape=jax.ShapeDtypeStruct((b, n, d), out_dtype),
        compiler_params=_params("parallel", "parallel"),
        name="normmod",
    )(x, g.reshape(1, d), shift, scale)


def _mod_kernel(c_ref, w_ref, b_ref, o_ref):
    c = c_ref[...]
    a = (c * jax.nn.sigmoid(c)).astype(BF16)
    o_ref[0] = jnp.dot(a, w_ref[0].astype(BF16), preferred_element_type=F32) + b_ref[0]


def modulation(c_rows, w_mod, b_mod):
    r, d = c_rows.shape
    nl, _, n = w_mod.shape
    tn = _tile(n, 1024)
    return pl.pallas_call(
        _mod_kernel,
        grid=(nl, n // tn),
        in_specs=[
            pl.BlockSpec((r, d), lambda l, j: (0, 0)),
            pl.BlockSpec((1, d, tn), lambda l, j: (l, 0, j)),
            pl.BlockSpec((1, 1, tn), lambda l, j: (l, 0, j)),
        ],
        out_specs=pl.BlockSpec((1, r, tn), lambda l, j: (l, 0, j)),
        out_shape=jax.ShapeDtypeStruct((nl, r, n), F32),
        compiler_params=_params("parallel", "parallel"),
        name="modulation",
    )(c_rows, w_mod, b_mod.reshape(nl, 1, n))


def _mm_kernel(a_ref, b_ref, o_ref):
    o_ref[...] = jnp.dot(a_ref[...], b_ref[...], preferred_element_type=F32).astype(o_ref.dtype)


def matmul(a, b, out_dtype):
    m, k = a.shape
    n = b.shape[1]
    tm, tn = _tile(m, 1024), _tile(n, 1024)
    return pl.pallas_call(
        _mm_kernel,
        grid=(m // tm, n // tn),
        in_specs=[
            pl.BlockSpec((tm, k), lambda i, j: (i, 0)),
            pl.BlockSpec((k, tn), lambda i, j: (0, j)),
        ],
        out_specs=pl.BlockSpec((tm, tn), lambda i, j: (i, j)),
        out_shape=jax.ShapeDtypeStruct((m, n), out_dtype),
        compiler_params=_params("parallel", "parallel"),
        name="matmul",
    )(a, b)


def _swap_halves(x):
    n = x.shape[-1]
    quarter = ATT_QK_DIM // 4
    lane = lax.broadcasted_iota(jnp.int32, x.shape, x.ndim - 1)
    up = pltpu.roll(x, n - quarter, x.ndim - 1)
    down = pltpu.roll(x, quarter, x.ndim - 1)
    return jnp.where(lane % (2 * quarter) < quarter, up, down)


def _rope_kernel(q_ref, k_ref, cos_ref, sin_ref, qo_ref, ko_ref):
    cos = cos_ref[...]
    sin = sin_ref[...]
    scale = ATT_QK_DIM ** -0.5 * math.log2(math.e)
    lane = lax.broadcasted_iota(jnp.int32, cos.shape, 1)
    first = lane < ATT_QK_DIM
    for h in range(ATT_HEADS):
        cols = slice(h * LANES, (h + 1) * LANES)
        q = q_ref[0, :, cols].astype(F32)
        k = k_ref[0, :, cols].astype(F32)
        qr = (q * cos + _swap_halves(q) * sin) * scale
        kr = k * cos + _swap_halves(k) * sin
        qo_ref[0, h, 0] = jnp.where(first, qr, 0.0).astype(qo_ref.dtype)
        qo_ref[0, h, 1] = jnp.where(first, 0.0, qr).astype(qo_ref.dtype)
        ko_ref[0, :, cols] = kr.astype(ko_ref.dtype)


def rope_qk(proj, cos, sin):
    b, n, _ = proj.shape
    tn = _tile(n, 512)
    w = ATT_WIDTH
    return pl.pallas_call(
        _rope_kernel,
        grid=(b, n // tn),
        in_specs=[
            pl.BlockSpec((1, tn, w), lambda bi, i: (bi, i, 0)),
            pl.BlockSpec((1, tn, w), lambda bi, i: (bi, i, 1)),
            pl.BlockSpec((tn, LANES), lambda bi, i: (i, 0)),
            pl.BlockSpec((tn, LANES), lambda bi, i: (i, 0)),
        ],
        out_specs=[
            pl.BlockSpec((1, ATT_HEADS, 2, tn, LANES), lambda bi, i: (bi, 0, 0, i, 0)),
            pl.BlockSpec((1, tn, w), lambda bi, i: (bi, i, 0)),
        ],
        out_shape=[
            jax.ShapeDtypeStruct((b, ATT_HEADS, 2, n, LANES), BF16),
            jax.ShapeDtypeStruct((b, n, w), BF16),
        ],
        compiler_params=_params("parallel", "parallel"),
        name="rope_qk",
    )(proj, proj, cos, sin)


def rope_tables(n):
    rows = n // GRID_W
    row = jnp.broadcast_to(jnp.arange(rows, dtype=jnp.int32)[:, None], (rows, GRID_W)).reshape(-1)
    col = jnp.broadcast_to(jnp.arange(GRID_W, dtype=jnp.int32)[None, :], (rows, GRID_W)).reshape(-1)
    half = ATT_QK_DIM // 2
    inv = ROPE_BASE ** (-jnp.arange(0, half, 2, dtype=F32) / half)
    ang_r = row.astype(F32)[:, None] * inv
    ang_c = col.astype(F32)[:, None] * inv
    cr, sr, cc, sc = jnp.cos(ang_r), jnp.sin(ang_r), jnp.cos(ang_c), jnp.sin(ang_c)
    cos = jnp.concatenate([cr, cr, cc, cc], axis=-1)
    sin = jnp.concatenate([-sr, sr, -sc, sc], axis=-1)
    reps = LANES // ATT_QK_DIM
    return jnp.tile(cos, (1, reps)), jnp.tile(sin, (1, reps))


def _attn_kernel(lam_ref, q_ref, kt_ref, v_ref, g_ref, o_ref, m_ref, acc_ref, alpha_ref, s_ref, p_ref,
                 *, nk, tk, tq, rb, unroll, last_valid, post_scale):
    rows_all = 2 * tq

    def scores(j, slot):
        q = q_ref[0, 0].reshape(rows_all, LANES)
        s_ref[slot] = jnp.dot(q, kt_ref[0, 0, j], preferred_element_type=F32)

    def weighted_values(j, slot):
        v = v_ref[0, pl.ds(pl.multiple_of(j * tk, tk), tk), :]
        vo = jnp.concatenate([v, jnp.ones((tk, LANES), BF16)], axis=1)
        alpha = alpha_ref[slot]
        acc_ref[...] = (jnp.concatenate([alpha, alpha], axis=1) * acc_ref[...]
                        + jnp.dot(p_ref[slot], vo, preferred_element_type=F32))

    def softmax(slot, valid=tk):
        for r in range(rows_all // rb):
            rows = pl.ds(r * rb, rb)
            s = s_ref[slot, rows, :]
            if valid < tk:
                s = jnp.where(lax.broadcasted_iota(jnp.int32, s.shape, 1) < valid, s, -jnp.inf)
            m_old = m_ref[rows, :]
            m_new = jnp.maximum(m_old, jnp.max(s, axis=-1, keepdims=True))
            alpha_ref[slot, rows, :] = jnp.exp2(m_old - m_new)
            p_ref[slot, rows, :] = jnp.exp2(s - m_new[:, :1]).astype(BF16)
            m_ref[rows, :] = m_new

    def stage(j, slot, first=False, last=False):
        if not last:
            scores(j + 1, 1 - slot)
        if not first:
            weighted_values(j - 1, 1 - slot)
        softmax(slot, last_valid if last else tk)

    def group(jj, carry):
        for t in range(unroll):
            stage(unroll * jj + 1 + t, (1 + t) % 2)
        return carry

    m_ref[...] = jnp.full(m_ref.shape, -jnp.inf, F32)
    acc_ref[...] = jnp.zeros(acc_ref.shape, F32)
    scores(0, 0)
    stage(0, 0, first=True, last=nk == 1)
    ngroups = (nk - 2) // unroll if nk >= 2 else 0
    lax.fori_loop(0, ngroups, group, 0)
    for j in range(1 + unroll * ngroups, nk):
        stage(j, j % 2, last=j == nk - 1)
    weighted_values(nk - 1, (nk - 1) % 2)

    acc = acc_ref[...]
    o = acc[:, :LANES] / acc[:, LANES:]
    o = o[:tq] - lam_ref[0, 0] * o[tq:]
    o = o * lax.rsqrt(jnp.mean(o * o, axis=-1, keepdims=True) + EPS)
    o_ref[0] = (o * g_ref[...] * post_scale).astype(o_ref.dtype)


def diff_attention(q, kt, v, n_valid, lam, subln_g, post_scale, tq):
    b, h, _, s, _ = q.shape
    nk, tk = kt.shape[2], kt.shape[4]
    nkeys = v.shape[1]
    rows_all = 2 * tq
    last_valid = n_valid - (nk - 1) * tk
    assert nkeys == nk * tk and 0 < last_valid <= tk
    kern = functools.partial(_attn_kernel, nk=nk, tk=tk, tq=tq, rb=_tile(rows_all, 64), unroll=2, last_valid=last_valid,
                             post_scale=post_scale)
    return pl.pallas_call(
        kern,
        grid=(b, h, s // tq),
        in_specs=[
            pl.BlockSpec(memory_space=pltpu.SMEM),
            pl.BlockSpec((1, 1, 2, tq, LANES), lambda bi, hi, i: (bi, hi, 0, i, 0)),
            pl.BlockSpec((1, 1, nk, LANES, tk), lambda bi, hi, i: (bi, hi, 0, 0, 0)),
            pl.BlockSpec((1, nkeys, LANES), lambda bi, hi, i: (bi, 0, hi)),
            pl.BlockSpec((1, LANES), lambda bi, hi, i: (0, 0)),
        ],
        out_specs=pl.BlockSpec((1, tq, LANES), lambda bi, hi, i: (bi, i, hi)),
        out_shape=jax.ShapeDtypeStruct((b, s, h * LANES), BF16),
        scratch_shapes=[
            pltpu.VMEM((rows_all, LANES), F32),
            pltpu.VMEM((rows_all, 2 * LANES), F32),
            pltpu.VMEM((2, rows_all, LANES), F32),
            pltpu.VMEM((2, rows_all, tk), F32),
            pltpu.VMEM((2, rows_all, tk), BF16),
        ],
        compiler_params=_params("parallel", "parallel", "parallel"),
        name="diff_attention",
    )(lam.reshape(1, 1), q, kt, v, subln_g.reshape(1, LANES))


def keys_transposed(k, tk):
    b, nkeys, _ = k.shape
    k = k.reshape(b, nkeys // tk, tk, ATT_HEADS, LANES)
    return k.transpose(0, 3, 1, 4, 2)


def _sgu_kernel(u_ref, v_ref, g_ref, b_ref, ws_ref, bs_ref, o_ref):
    v = jax.nn.gelu(v_ref[0].astype(F32))
    mu = jnp.mean(v, axis=-1, keepdims=True)
    var = jnp.mean(jnp.square(v - mu), axis=-1, keepdims=True)
    v = ((v - mu) * lax.rsqrt(var + EPS) * g_ref[...] + b_ref[...]).astype(BF16)
    tn = v.shape[0]
    cw = SGU_WIDTH // SGU_GROUPS
    for c in range(tn // SGU_CHUNK):
        rows = slice(c * SGU_CHUNK, (c + 1) * SGU_CHUNK)
        for g in range(SGU_GROUPS):
            cols = slice(g * cw, (g + 1) * cw)
            mixed = jnp.dot(ws_ref[g], v[rows, cols], preferred_element_type=F32) + bs_ref[g]
            u = jax.nn.gelu(u_ref[0, rows, cols].astype(F32))
            o_ref[0, rows, cols] = (u * mixed).astype(o_ref.dtype)


def spatial_gating(proj, ln_g, ln_b, w_s, b_s):
    b, n, _ = proj.shape
    tn = _tile(n, 512)
    w = SGU_WIDTH
    u_blk = 3 * ATT_WIDTH // w
    return pl.pallas_call(
        _sgu_kernel,
        grid=(b, n // tn),
        in_specs=[
            pl.BlockSpec((1, tn, w), lambda bi, i: (bi, i, u_blk)),
            pl.BlockSpec((1, tn, w), lambda bi, i: (bi, i, u_blk + 1)),
            pl.BlockSpec((1, w), lambda bi, i: (0, 0)),
            pl.BlockSpec((1, w), lambda bi, i: (0, 0)),
            pl.BlockSpec((SGU_GROUPS, SGU_CHUNK, SGU_CHUNK), lambda bi, i: (0, 0, 0)),
            pl.BlockSpec((SGU_GROUPS, SGU_CHUNK, 1), lambda bi, i: (0, 0, 0)),
        ],
        out_specs=pl.BlockSpec((1, tn, w), lambda bi, i: (bi, i, 0)),
        out_shape=jax.ShapeDtypeStruct((b, n, w), BF16),
        compiler_params=_params("parallel", "parallel"),
        name="spatial_gating",
    )(proj, proj, ln_g.reshape(1, w), ln_b.reshape(1, w), w_s.astype(BF16),
      b_s.reshape(SGU_GROUPS, SGU_CHUNK, 1))


def _glu(a_ref, g_ref):
    return a_ref[0].astype(F32) * jax.nn.sigmoid(g_ref[0].astype(F32))


def _conv_kernel(a_ref, g_ref, ap_ref, gp_ref, an_ref, gn_ref, w_ref, b_ref, lg_ref, lb_ref, o_ref, h_ref, *, rc):
    i = pl.program_id(1)
    tn = a_ref.shape[1]
    h_ref[pl.ds(HALO, tn), :] = _glu(a_ref, g_ref)
    h_ref[pl.ds(0, HALO), :] = jnp.where(i > 0, _glu(ap_ref, gp_ref), 0.0)
    h_ref[pl.ds(HALO + tn, HALO), :] = jnp.where(i < pl.num_programs(1) - 1, _glu(an_ref, gn_ref), 0.0)
    first = HALO - CONV_K // 2
    wrows = rc + 2 * HALO
    sub = 8

    def chunk(r, carry):
        r0 = pl.multiple_of(r * rc, rc)
        cols = []
        for c in range(CONV_WIDTH // LANES):
            lanes = pl.ds(c * LANES, LANES)
            win = h_ref[pl.ds(r0, wrows), lanes]
            acc = jnp.zeros((rc, LANES), F32) + b_ref[:, lanes]
            for rot in range(sub):
                shifted = win if rot == 0 else pltpu.roll(win, wrows - rot, 0)
                for k in range(CONV_K):
                    off = first + k
                    if off % sub == rot:
                        base = off - rot
                        acc = acc + w_ref[pl.ds(k, 1), lanes] * shifted[base:base + rc]
            cols.append(acc)
        acc = jnp.concatenate(cols, axis=1)
        mu = jnp.mean(acc, axis=-1, keepdims=True)
        var = jnp.mean(jnp.square(acc - mu), axis=-1, keepdims=True)
        y = (acc - mu) * lax.rsqrt(var + EPS) * lg_ref[...] + lb_ref[...]
        o_ref[0, pl.ds(r0, rc), :] = (y * jax.nn.sigmoid(y)).astype(o_ref.dtype)
        return carry

    lax.fori_loop(0, tn // rc, chunk, 0)


def conformer_conv(proj, w_dw, b_dw, ln_g, ln_b):
    b, n, _ = proj.shape
    tn = _tile(n, 512)
    w = CONV_WIDTH
    a_blk = (3 * ATT_WIDTH + 2 * SGU_WIDTH) // w
    nh = n // HALO
    per = tn // HALO

    def main(c):
        return pl.BlockSpec((1, tn, w), lambda bi, i: (bi, i, c))

    def prev(c):
        return pl.BlockSpec((1, HALO, w), lambda bi, i: (bi, jnp.maximum(i * per - 1, 0), c))

    def nxt(c):
        return pl.BlockSpec((1, HALO, w), lambda bi, i: (bi, jnp.minimum((i + 1) * per, nh - 1), c))

    vec = pl.BlockSpec((1, w), lambda bi, i: (0, 0))
    return pl.pallas_call(
        functools.partial(_conv_kernel, rc=_tile(tn, 64)),
        grid=(b, n // tn),
        in_specs=[main(a_blk), main(a_blk + 1), prev(a_blk), prev(a_blk + 1), nxt(a_blk), nxt(a_blk + 1),
                  pl.BlockSpec((CONV_K, w), lambda bi, i: (0, 0)), vec, vec, vec],
        out_specs=pl.BlockSpec((1, tn, w), lambda bi, i: (bi, i, 0)),
        out_shape=jax.ShapeDtypeStruct((b, n, w), BF16),
        scratch_shapes=[pltpu.VMEM((tn + 2 * HALO, w), F32)],
        compiler_params=_params("parallel", "parallel"),
        name="conformer_conv",
    )(proj, proj, proj, proj, proj, proj, w_dw, b_dw.reshape(1, w), ln_g.reshape(1, w), ln_b.reshape(1, w))


def _merge_kernel(h_ref, att_ref, sgu_ref, conv_ref, wg0, wg1, wg2, bg0, bg1, bg2, wa, ws, wc, o_ref):
    h = h_ref[...]
    y = None
    for br_ref, wg, bg, wo in ((att_ref, wg0, bg0, wa), (sgu_ref, wg1, bg1, ws), (conv_ref, wg2, bg2, wc)):
        gate = jax.nn.sigmoid(jnp.dot(h, wg[...], preferred_element_type=F32) + bg[...])
        t = gate * jnp.dot(br_ref[...], wo[...], preferred_element_type=F32)
        y = t if y is None else y + t
    o_ref[...] = y.astype(o_ref.dtype)


def merge_branches(h, att, sgu, conv, w_gate, b_gate, w_att_out, w_sgu_out, w_conv_out):
    m, d = h.shape
    tm, tn = _tile(m, 512), _tile(d, 512)
    nj = d // tn
    bw = att.shape[1]

    def wg(br):
        return pl.BlockSpec((d, tn), lambda i, j: (0, br * nj + j))

    def bg(br):
        return pl.BlockSpec((1, tn), lambda i, j: (0, br * nj + j))

    row = pl.BlockSpec((tm, bw), lambda i, j: (i, 0))
    wout = pl.BlockSpec((bw, tn), lambda i, j: (0, j))
    b_gate = b_gate.reshape(1, N_BRANCH * d)
    return pl.pallas_call(
        _merge_kernel,
        grid=(m // tm, nj),
        in_specs=[pl.BlockSpec((tm, d), lambda i, j: (i, 0)), row, row, row,
                  wg(0), wg(1), wg(2), bg(0), bg(1), bg(2), wout, wout, wout],
        out_specs=pl.BlockSpec((tm, tn), lambda i, j: (i, j)),
        out_shape=jax.ShapeDtypeStruct((m, d), BF16),
        compiler_params=_params("parallel", "parallel"),
        name="merge_branches",
    )(h, att, sgu, conv, w_gate, w_gate, w_gate, b_gate, b_gate, b_gate, w_att_out, w_sgu_out, w_conv_out)


def _proj_res_kernel(y_ref, w_ref, x_ref, g_ref, ng_ref, nsh_ref, nsc_ref, o_ref, hn_ref):
    x_new = x_ref[...] + g_ref[0] * jnp.dot(y_ref[...], w_ref[...], preferred_element_type=F32)
    _store_residual(x_new, o_ref, hn_ref, ng_ref, nsh_ref, nsc_ref)


def _next_norm_args(norm, d):
    g, shift, scale = norm
    return g.reshape(1, d), shift, scale


def proj_residual(y, w, x, gate, rows_per_batch, norm, h_dtype):
    m, k = y.shape
    d = w.shape[1]
    tm = _tile(rows_per_batch, 512)
    per = rows_per_batch // tm
    row = pl.BlockSpec((tm, d), lambda i: (i, 0))
    per_batch = pl.BlockSpec((1, 1, d), lambda i: (i // per, 0, 0))
    return pl.pallas_call(
        _proj_res_kernel,
        grid=(m // tm,),
        in_specs=[
            pl.BlockSpec((tm, k), lambda i: (i, 0)),
            pl.BlockSpec((k, d), lambda i: (0, 0)),
            row, per_batch,
            pl.BlockSpec((1, d), lambda i: (0, 0)), per_batch, per_batch,
        ],
        out_specs=[row, row],
        out_shape=[jax.ShapeDtypeStruct((m, d), F32), jax.ShapeDtypeStruct((m, d), h_dtype)],
        compiler_params=_params("parallel"),
        name="proj_residual",
    )(y, w, x, gate, *_next_norm_args(norm, d))


def _swiglu_out(h_ref, w1_ref, w3_ref, w2_ref, parts=1):
    rows = h_ref.shape[0] // parts
    outs = []
    for r in range(parts):
        h = h_ref[pl.ds(r * rows, rows), :]
        a = jnp.dot(h, w1_ref[0], preferred_element_type=F32)
        b = jnp.dot(h, w3_ref[0], preferred_element_type=F32)
        z = (a * jax.nn.sigmoid(a) * b).astype(BF16)
        outs.append(jnp.dot(z, w2_ref[0], preferred_element_type=F32))
    return jnp.concatenate(outs, axis=0)


def _ffn_kernel(h_ref, w1_ref, w3_ref, w2_ref, x_ref, g_ref, ng_ref, nsh_ref, nsc_ref, o_ref, hn_ref, acc_ref):
    f = pl.program_id(1)

    @pl.when(f == 0)
    def _():
        acc_ref[...] = jnp.zeros(acc_ref.shape, F32)

    acc_ref[...] += _swiglu_out(h_ref, w1_ref, w3_ref, w2_ref)

    @pl.when(f == pl.num_programs(1) - 1)
    def _():
        _store_residual(x_ref[...] + g_ref[0] * acc_ref[...], o_ref, hn_ref, ng_ref, nsh_ref, nsc_ref)


def ffn_residual(h, w1, w3, w2, x, gate, rows_per_batch, norm, h_dtype):
    m, d = h.shape
    fdim = w1.shape[2]
    tm = _tile(rows_per_batch, 512)
    tf = _tile(fdim, 512)
    per = rows_per_batch // tm
    row = pl.BlockSpec((tm, d), lambda i, f: (i, 0))
    per_batch = pl.BlockSpec((1, 1, d), lambda i, f: (i // per, 0, 0))
    return pl.pallas_call(
        _ffn_kernel,
        grid=(m // tm, fdim // tf),
        in_specs=[
            row,
            pl.BlockSpec((1, d, tf), lambda i, f: (0, 0, f)),
            pl.BlockSpec((1, d, tf), lambda i, f: (0, 0, f)),
            pl.BlockSpec((1, tf, d), lambda i, f: (0, f, 0)),
            row, per_batch,
            pl.BlockSpec((1, d), lambda i, f: (0, 0)), per_batch, per_batch,
        ],
        out_specs=[row, row],
        out_shape=[jax.ShapeDtypeStruct((m, d), F32), jax.ShapeDtypeStruct((m, d), h_dtype)],
        scratch_shapes=[pltpu.VMEM((tm, d), F32)],
        compiler_params=_params("parallel", "arbitrary"),
        name="ffn_residual",
    )(h, w1, w3, w2, x, gate, *_next_norm_args(norm, d))


def _router_kernel(h_ref, w_ref, b_ref, sel_ref, wts_ref):
    logits = jnp.dot(h_ref[...].astype(BF16), w_ref[...], preferred_element_type=F32) + b_ref[...]
    ne = logits.shape[-1]
    idx = lax.broadcasted_iota(jnp.int32, logits.shape, 1)
    v1 = jnp.max(logits, axis=-1, keepdims=True)
    i1 = jnp.min(jnp.where(logits == v1, idx, ne), axis=-1, keepdims=True)
    rest = jnp.where(idx == i1, -jnp.inf, logits)
    v2 = jnp.max(rest, axis=-1, keepdims=True)
    i2 = jnp.min(jnp.where(rest == v2, idx, ne), axis=-1, keepdims=True)
    e2 = jnp.exp(v2 - v1)
    sel_ref[...] = jnp.concatenate([i1, i2], axis=1)
    wts_ref[...] = jnp.concatenate([1.0 / (1.0 + e2), e2 / (1.0 + e2)], axis=1)


def router(h, w_r, b_r):
    m, d = h.shape
    ne = w_r.shape[1]
    tm = _tile(m, 512)
    return pl.pallas_call(
        _router_kernel,
        grid=(m // tm,),
        in_specs=[
            pl.BlockSpec((tm, d), lambda i: (i, 0)),
            pl.BlockSpec((d, ne), lambda i: (0, 0)),
            pl.BlockSpec((1, ne), lambda i: (0, 0)),
        ],
        out_specs=[pl.BlockSpec((tm, TOP_K), lambda i: (i, 0)), pl.BlockSpec((tm, TOP_K), lambda i: (i, 0))],
        out_shape=[jax.ShapeDtypeStruct((m, TOP_K), jnp.int32), jax.ShapeDtypeStruct((m, TOP_K), F32)],
        compiler_params=_params("parallel"),
        name="router",
    )(h, w_r.astype(BF16), b_r.reshape(1, ne))


def dispatch_plan(sel, tm):
    m = sel.shape[0]
    ne = N_EXPERTS
    npairs = m * TOP_K
    rows = npairs + ne * tm
    e_flat = sel.reshape(npairs)
    onehot = (e_flat[:, None] == jnp.arange(ne, dtype=jnp.int32)[None, :]).astype(jnp.int32)
    rank = jnp.cumsum(onehot, axis=0) - onehot
    counts = jnp.sum(onehot, axis=0)
    padded = (counts + tm - 1) // tm * tm
    ends = jnp.cumsum(padded)
    pos = (ends - padded)[e_flat] + jnp.sum(rank * onehot, axis=1)
    tile_start = jnp.arange(rows // tm, dtype=jnp.int32) * tm
    tile_expert = jnp.minimum(jnp.searchsorted(ends, tile_start, side="right"), ne - 1).astype(jnp.int32)
    meta = jnp.concatenate([tile_expert, (ends[-1:] // tm).astype(jnp.int32)])
    return pos.reshape(m, TOP_K), meta, rows


def _wait_rows(src_ref, dst_ref, sem, n):
    pltpu.make_async_copy(src_ref.at[pl.ds(0, n)], dst_ref.at[pl.ds(0, n)], sem).wait()


def _scatter_rows_kernel(pos_ref, h_ref, init_ref, o_ref, sem):
    del init_ref
    tm = h_ref.shape[0]

    def issue(r, carry):
        for k in range(TOP_K):
            pltpu.make_async_copy(h_ref.at[pl.ds(r, 1)], o_ref.at[pl.ds(pos_ref[0, k, r], 1)], sem).start()
        return carry

    lax.fori_loop(0, tm, issue, 0, unroll=8)
    for k in range(TOP_K):
        _wait_rows(h_ref, o_ref, sem, tm)


def scatter_rows(h, pos, rows, tm):
    m, d = h.shape
    pos_t = pos.reshape(m // tm, tm, TOP_K).transpose(0, 2, 1)
    return pl.pallas_call(
        _scatter_rows_kernel,
        grid=(m // tm,),
        in_specs=[
            pl.BlockSpec((1, TOP_K, tm), lambda i: (i, 0, 0), memory_space=pltpu.SMEM),
            pl.BlockSpec((tm, d), lambda i: (i, 0)),
            pl.BlockSpec(memory_space=pl.ANY),
        ],
        out_specs=pl.BlockSpec(memory_space=pl.ANY),
        out_shape=jax.ShapeDtypeStruct((rows, d), h.dtype),
        scratch_shapes=[pltpu.SemaphoreType.DMA(())],
        input_output_aliases={2: 0},
        compiler_params=_params("arbitrary"),
        name="scatter_rows",
    )(pos_t, h, jnp.zeros((rows, d), h.dtype))


def _expert_ffn_kernel(meta_ref, h_ref, w1_ref, w3_ref, w2_ref, o_ref, hb_ref):
    i, f = pl.program_id(0), pl.program_id(1)
    used = i < meta_ref[meta_ref.shape[0] - 1]

    @pl.when(f == 0)
    def _():
        hb_ref[...] = h_ref[...].astype(BF16)
        o_ref[...] = jnp.zeros(o_ref.shape, o_ref.dtype)

    @pl.when(used)
    def _():
        o_ref[...] += _swiglu_out(hb_ref, w1_ref, w3_ref, w2_ref)


def expert_ffn(hs, w1, w3, w2, meta, tm):
    r, d = hs.shape
    fdim = w1.shape[2]
    tf = _tile(fdim, 256)
    grid_spec = pltpu.PrefetchScalarGridSpec(
        num_scalar_prefetch=1,
        grid=(r // tm, fdim // tf),
        in_specs=[
            pl.BlockSpec((tm, d), lambda i, f, meta: (i, 0)),
            pl.BlockSpec((1, d, tf), lambda i, f, meta: (meta[i], 0, f)),
            pl.BlockSpec((1, d, tf), lambda i, f, meta: (meta[i], 0, f)),
            pl.BlockSpec((1, tf, d), lambda i, f, meta: (meta[i], f, 0)),
        ],
        out_specs=pl.BlockSpec((tm, d), lambda i, f, meta: (i, 0)),
        scratch_shapes=[pltpu.VMEM((tm, d), BF16)],
    )
    return pl.pallas_call(
        _expert_ffn_kernel,
        grid_spec=grid_spec,
        out_shape=jax.ShapeDtypeStruct((r, d), F32),
        compiler_params=_params("parallel", "arbitrary"),
        name="expert_ffn",
    )(meta, hs, w1, w3, w2)


def _combine_kernel(pos_ref, ys_ref, w_ref, x_ref, g_ref, ng_ref, nsh_ref, nsc_ref, o_ref, hn_ref, buf_ref, sem):
    tm = x_ref.shape[0]

    def issue(r, carry):
        for k in range(TOP_K):
            pltpu.make_async_copy(ys_ref.at[pl.ds(pos_ref[0, k, r], 1)], buf_ref.at[pl.ds(k * tm + r, 1)], sem).start()
        return carry

    lax.fori_loop(0, tm, issue, 0, unroll=8)
    _wait_rows(ys_ref, buf_ref, sem, TOP_K * tm)
    w = w_ref[...]
    y = w[:, 0:1] * buf_ref[pl.ds(0, tm), :]
    for k in range(1, TOP_K):
        y = y + w[:, k:k + 1] * buf_ref[pl.ds(k * tm, tm), :]
    _store_residual(x_ref[...] + g_ref[0] * y, o_ref, hn_ref, ng_ref, nsh_ref, nsc_ref)


def combine_residual(ys, pos, wts, x, gate, rows_per_batch, norm, h_dtype):
    m, d = x.shape
    tm = _tile(rows_per_batch, 256)
    per = rows_per_batch // tm
    pos_t = pos.reshape(m // tm, tm, TOP_K).transpose(0, 2, 1)
    row = pl.BlockSpec((tm, d), lambda i: (i, 0))
    per_batch = pl.BlockSpec((1, 1, d), lambda i: (i // per, 0, 0))
    return pl.pallas_call(
        _combine_kernel,
        grid=(m // tm,),
        in_specs=[
            pl.BlockSpec((1, TOP_K, tm), lambda i: (i, 0, 0), memory_space=pltpu.SMEM),
            pl.BlockSpec(memory_space=pl.ANY),
            pl.BlockSpec((tm, TOP_K), lambda i: (i, 0)),
            row, per_batch,
            pl.BlockSpec((1, d), lambda i: (0, 0)), per_batch, per_batch,
        ],
        out_specs=[row, row],
        out_shape=[jax.ShapeDtypeStruct((m, d), F32), jax.ShapeDtypeStruct((m, d), h_dtype)],
        scratch_shapes=[pltpu.VMEM((TOP_K * tm, d), F32), pltpu.SemaphoreType.DMA(())],
        compiler_params=_params("arbitrary"),
        name="combine_residual",
    )(pos_t, ys, wts, x, gate, *_next_norm_args(norm, d))


def moe_residual(h, w_r, b_r, w1, w3, w2, x, gate, rows_per_batch, norm, h_dtype):
    m = h.shape[0]
    tm = 1024 if m * TOP_K >= 8 * 1024 else 128
    sel, wts = router(h, w_r, b_r)
    pos, meta, rows = dispatch_plan(sel, tm)
    hs = scatter_rows(h, pos, rows, _tile(rows_per_batch, 512))
    ys = expert_ffn(hs, w1, w3, w2, meta, tm)
    return combine_residual(ys, pos, wts, x, gate, rows_per_batch, norm, h_dtype)


def _pad_keys(parts, tk):
    n = sum(t.shape[1] for t in parts)
    pad = -n % tk
    if pad:
        parts = parts + [jnp.zeros((parts[0].shape[0], pad, parts[0].shape[2]), parts[0].dtype)]
    return jnp.concatenate(parts, axis=1) if len(parts) > 1 else parts[0]


def _mixer(x, n, h, proj, keys, values, lam, lam_init, tq, tk, q, p):
    m, d = x.shape
    n_keys = sum(t.shape[1] for t in keys)
    att = diff_attention(q, keys_transposed(_pad_keys(keys, tk), tk), _pad_keys(values, tk), n_keys, lam,
                         p["subln_g"], 1 - lam_init, tq)
    sgu = spatial_gating(proj, p["sgu_ln_g"], p["sgu_ln_b"], p["w_spatial"], p["b_spatial"])
    conv = conformer_conv(proj, p["conv_w"], p["conv_b"], p["conv_ln_g"], p["conv_ln_b"])
    y = merge_branches(h, att.reshape(m, -1), sgu.reshape(m, -1), conv.reshape(m, -1),
                       p["w_gate"], p["b_gate"], p["w_att_out"], p["w_sgu_out"], p["w_conv_out"])
    norm2 = (p["norm2_g"], p["sh2"], p["sc2"])
    return proj_residual(y, p["w_o"], x, p["g1"], n, norm2, F32 if p["moe"] else BF16)


def _channel(x, h, n, p, next_norm, next_dtype):
    if p["moe"]:
        return moe_residual(h, p["router_w"], p["router_b"], p["w1"], p["w3"], p["w2"], x, p["g2"], n,
                            next_norm, next_dtype)
    return ffn_residual(h, p["w1"], p["w3"], p["w2"], x, p["g2"], n, next_norm, next_dtype)


def kernel(x, c, ctx, c_ctx, w_mod, b_mod, norm1_g, norm2_g, w_in, lam_q1, lam_k1, lam_q2, lam_k2, subln_g, w_att_out, sgu_ln_g, sgu_ln_b, w_spatial, b_spatial, w_sgu_out, conv_w, conv_b, conv_ln_g, conv_ln_b, w_conv_out, w_gate, b_gate, w_o, ffn_w1, ffn_w3, ffn_w2, router_w, router_b, moe_w1, moe_w3, moe_w2, final_g):
    b, s, d = x.shape
    nc = ctx.shape[1]
    depth = w_in.shape[0]
    cos, sin = rope_tables(s)
    ones, zeros = jnp.ones((nc, LANES), F32), jnp.zeros((nc, LANES), F32)

    c_rows = jnp.concatenate([c, c_ctx[None, :], jnp.zeros((SUBLANES - b - 1, d), F32)], axis=0)
    mods = modulation(c_rows, w_mod, b_mod)
    lats = [[t[:, None, :] for t in jnp.split(mods[i, :b], 6, axis=-1)] for i in range(depth)]
    cxs = [[jnp.broadcast_to(t[:, None, :], (b, 1, d)) for t in jnp.split(mods[i, b:b + 1], 6, axis=-1)]
           for i in range(depth)]
    zero = jnp.zeros((b, 1, d), F32)

    x = x.reshape(b * s, d)
    xc = ctx.reshape(b * nc, d)
    h_lat = normmod(x.reshape(b, s, d), norm1_g[0], lats[0][0], lats[0][1], BF16).reshape(b * s, d)
    h_ctx = normmod(ctx, norm1_g[0], cxs[0][0], cxs[0][1], BF16).reshape(b * nc, d)
    for i in range(depth):
        last = i == depth - 1
        lat, cx = lats[i], cxs[i]
        lam_init = 0.8 - 0.6 * math.exp(-0.3 * i)
        lam = (jnp.exp(jnp.sum(lam_q1[i] * lam_k1[i]).astype(F32))
               - jnp.exp(jnp.sum(lam_q2[i] * lam_k2[i]).astype(F32)) + lam_init)
        shared = dict(
            norm2_g=norm2_g[i], subln_g=subln_g[i], sgu_ln_g=sgu_ln_g[i], sgu_ln_b=sgu_ln_b[i],
            w_spatial=w_spatial[i], b_spatial=b_spatial[i], conv_w=conv_w[i], conv_b=conv_b[i],
            conv_ln_g=conv_ln_g[i], conv_ln_b=conv_ln_b[i],
            w_gate=w_gate[i].astype(BF16), b_gate=b_gate[i], w_att_out=w_att_out[i].astype(BF16),
            w_sgu_out=w_sgu_out[i].astype(BF16), w_conv_out=w_conv_out[i].astype(BF16), w_o=w_o[i].astype(BF16),
            moe=i % 2 == 1)
        j = i // 2
        if i % 2 == 0:
            shared.update(w1=ffn_w1[j][None].astype(BF16), w3=ffn_w3[j][None].astype(BF16),
                          w2=ffn_w2[j][None].astype(BF16))
        else:
            shared.update(w1=moe_w1[j].astype(BF16), w3=moe_w3[j].astype(BF16), w2=moe_w2[j].astype(BF16),
                          router_w=router_w[j], router_b=router_b[j])
        p_lat = dict(shared, g1=lat[2], sh2=lat[3], sc2=lat[4], g2=lat[5])
        p_ctx = dict(shared, g1=cx[2], sh2=cx[3], sc2=cx[4], g2=cx[5])
        w_in_i = w_in[i].astype(BF16)

        proj_l = matmul(h_lat, w_in_i, BF16).reshape(b, s, IN_COLS)
        proj_c = matmul(h_ctx, w_in_i, BF16).reshape(b, nc, IN_COLS)
        rope_l = rope_qk(proj_l, cos, sin)
        rope_c = rope_qk(proj_c, ones, zeros)
        v_l = proj_l[..., 2 * ATT_WIDTH:3 * ATT_WIDTH]
        v_c = proj_c[..., 2 * ATT_WIDTH:3 * ATT_WIDTH]
        x, h2_lat = _mixer(x, s, h_lat, proj_l, [rope_l[1], rope_c[1]], [v_l, v_c], lam, lam_init,
                           _tile(s, 512), ATT_KEY_TILE, rope_l[0], p_lat)
        if not last:
            xc, h2_ctx = _mixer(xc, nc, h_ctx, proj_c, [rope_c[1]], [v_c], lam, lam_init, nc, nc, rope_c[0], p_ctx)

        if last:
            _, out = _channel(x, h2_lat, s, p_lat, (final_g, zero, zero), F32)
            return out.reshape(b, s, d)
        x, h_lat = _channel(x, h2_lat, s, p_lat, (norm1_g[i + 1], lats[i + 1][0], lats[i + 1][1]), BF16)
        xc, h_ctx = _channel(xc, h2_ctx, nc, p_ctx, (norm1_g[i + 1], cxs[i + 1][0], cxs[i + 1][1]), BF16)
```

```python
import functools
import math

import jax
import jax.numpy as jnp
from jax import lax
from jax.experimental import pallas as pl
from jax.experimental.pallas import tpu as pltpu

D_MODEL = 2048
BATCH = 2
SEQ = 16384
DEPTH = 4
GRID_W = 64
CTX_LEN = 256
ATT_HEADS = 8
ATT_QK_DIM = 64
ATT_V_DIM = 2 * ATT_QK_DIM
ATT_WIDTH = ATT_HEADS * ATT_V_DIM
ROPE_BASE = 10000.0
SGU_CHUNK = 128
SGU_GROUPS = 8
SGU_WIDTH = 1024
CONV_WIDTH = 1024
CONV_K = 31
IN_COLS = 3 * ATT_WIDTH + 2 * SGU_WIDTH + 2 * CONV_WIDTH
N_BRANCH = 3
FFN_DIM = 5632
N_EXPERTS = 8
TOP_K = 2
EXPERT_DIM = 2816
EPS = 1e-6

LANES = 128
SUBLANES = 8
ATT_KEY_TILE = 640
HALO = 16
VMEM_LIMIT = 56 * 1024 * 1024
BF16 = jnp.bfloat16
F32 = jnp.float32


def _params(*sem):
    return pltpu.CompilerParams(dimension_semantics=sem, vmem_limit_bytes=VMEM_LIMIT)


def _tile(n, pref):
    t = min(pref, n)
    while n % t:
        t //= 2
    return t


def _rownorm_mod(x, g, shift, scale):
    y = x * lax.rsqrt(jnp.mean(x * x, axis=-1, keepdims=True) + EPS)
    return y * g * (1 + scale) + shift


def _normmod_kernel(x_ref, g_ref, sh_ref, sc_ref, o_ref):
    o_ref[0] = _rownorm_mod(x_ref[0], g_ref[...], sh_ref[0], sc_ref[0]).astype(o_ref.dtype)


def _store_residual(x_new, o_ref, hn_ref, ng_ref, nsh_ref, nsc_ref):
    o_ref[...] = x_new
    hn_ref[...] = _rownorm_mod(x_new, ng_ref[...], nsh_ref[0], nsc_ref[0]).astype(hn_ref.dtype)


def normmod(x, g, shift, scale, out_dtype):
    b, n, d = x.shape
    tn = _tile(n, 512)
    return pl.pallas_call(
        _normmod_kernel,
        grid=(b, n // tn),
        in_specs=[
            pl.BlockSpec((1, tn, d), lambda bi, i: (bi, i, 0)),
            pl.BlockSpec((1, d), lambda bi, i: (0, 0)),
            pl.BlockSpec((1, 1, d), lambda bi, i: (bi, 0, 0)),
            pl.BlockSpec((1, 1, d), lambda bi, i: (bi, 0, 0)),
        ],
        out_specs=pl.BlockSpec((1, tn, d), lambda bi, i: (bi, i, 0)),
        out_shape=jax.ShapeDtypeStruct((b, n, d), out_dtype),
        compiler_params=_params("parallel", "parallel"),
        name="normmod",
    )(x, g.reshape(1, d), shift, scale)


def _mod_kernel(c_ref, w_ref, b_ref, o_ref):
    c = c_ref[...]
    a = (c * jax.nn.sigmoid(c)).astype(BF16)
    o_ref[0] = jnp.dot(a, w_ref[0].astype(BF16), preferred_element_type=F32) + b_ref[0]


def modulation(c_rows, w_mod, b_mod):
    r, d = c_rows.shape
    nl, _, n = w_mod.shape
    tn = _tile(n, 1024)
    return pl.pallas_call(
        _mod_kernel,
        grid=(nl, n // tn),
        in_specs=[
            pl.BlockSpec((r, d), lambda l, j: (0, 0)),
            pl.BlockSpec((1, d, tn), lambda l, j: (l, 0, j)),
            pl.BlockSpec((1, 1, tn), lambda l, j: (l, 0, j)),
        ],
        out_specs=pl.BlockSpec((1, r, tn), lambda l, j: (l, 0, j)),
        out_shape=jax.ShapeDtypeStruct((nl, r, n), F32),
        compiler_params=_params("parallel", "parallel"),
        name="modulation",
    )(c_rows, w_mod, b_mod.reshape(nl, 1, n))


def _mm_kernel(a_ref, b_ref, o_ref):
    o_ref[...] = jnp.dot(a_ref[...], b_ref[...], preferred_element_type=F32).astype(o_ref.dtype)


def matmul(a, b, out_dtype):
    m, k = a.shape
    n = b.shape[1]
    tm, tn = _tile(m, 2048), _tile(n, 1024)
    return pl.pallas_call(
        _mm_kernel,
        grid=(m // tm, n // tn),
        in_specs=[
            pl.BlockSpec((tm, k), lambda i, j: (i, 0)),
            pl.BlockSpec((k, tn), lambda i, j: (0, j)),
        ],
        out_specs=pl.BlockSpec((tm, tn), lambda i, j: (i, j)),
        out_shape=jax.ShapeDtypeStruct((m, n), out_dtype),
        compiler_params=_params("parallel", "parallel"),
        name="matmul",
    )(a, b)


def _swap_halves(x):
    n = x.shape[-1]
    quarter = ATT_QK_DIM // 4
    lane = lax.broadcasted_iota(jnp.int32, x.shape, x.ndim - 1)
    up = pltpu.roll(x, n - quarter, x.ndim - 1)
    down = pltpu.roll(x, quarter, x.ndim - 1)
    return jnp.where(lane % (2 * quarter) < quarter, up, down)


def _rope_kernel(q_ref, k_ref, cos_ref, sin_ref, qo_ref, ko_ref):
    cos = cos_ref[...]
    sin = sin_ref[...]
    scale = ATT_QK_DIM ** -0.5 * math.log2(math.e)
    lane = lax.broadcasted_iota(jnp.int32, cos.shape, 1)
    first = lane < ATT_QK_DIM
    for h in range(ATT_HEADS):
        cols = slice(h * LANES, (h + 1) * LANES)
        q = q_ref[0, :, cols].astype(F32)
        k = k_ref[0, :, cols].astype(F32)
        qr = (q * cos + _swap_halves(q) * sin) * scale
        kr = k * cos + _swap_halves(k) * sin
        qo_ref[0, h, 0] = jnp.where(first, qr, 0.0).astype(qo_ref.dtype)
        qo_ref[0, h, 1] = jnp.where(first, 0.0, qr).astype(qo_ref.dtype)
        ko_ref[0, :, cols] = kr.astype(ko_ref.dtype)


def rope_qk(proj, cos, sin):
    b, n, _ = proj.shape
    tn = _tile(n, 512)
    w = ATT_WIDTH
    return pl.pallas_call(
        _rope_kernel,
        grid=(b, n // tn),
        in_specs=[
            pl.BlockSpec((1, tn, w), lambda bi, i: (bi, i, 0)),
            pl.BlockSpec((1, tn, w), lambda bi, i: (bi, i, 1)),
            pl.BlockSpec((tn, LANES), lambda bi, i: (i, 0)),
            pl.BlockSpec((tn, LANES), lambda bi, i: (i, 0)),
        ],
        out_specs=[
            pl.BlockSpec((1, ATT_HEADS, 2, tn, LANES), lambda bi, i: (bi, 0, 0, i, 0)),
            pl.BlockSpec((1, tn, w), lambda bi, i: (bi, i, 0)),
        ],
        out_shape=[
            jax.ShapeDtypeStruct((b, ATT_HEADS, 2, n, LANES), BF16),
            jax.ShapeDtypeStruct((b, n, w), BF16),
        ],
        compiler_params=_params("parallel", "parallel"),
        name="rope_qk",
    )(proj, proj, cos, sin)


def rope_tables(n):
    rows = n // GRID_W
    row = jnp.broadcast_to(jnp.arange(rows, dtype=jnp.int32)[:, None], (rows, GRID_W)).reshape(-1)
    col = jnp.broadcast_to(jnp.arange(GRID_W, dtype=jnp.int32)[None, :], (rows, GRID_W)).reshape(-1)
    half = ATT_QK_DIM // 2
    inv = ROPE_BASE ** (-jnp.arange(0, half, 2, dtype=F32) / half)
    ang_r = row.astype(F32)[:, None] * inv
    ang_c = col.astype(F32)[:, None] * inv
    cr, sr, cc, sc = jnp.cos(ang_r), jnp.sin(ang_r), jnp.cos(ang_c), jnp.sin(ang_c)
    cos = jnp.concatenate([cr, cr, cc, cc], axis=-1)
    sin = jnp.concatenate([-sr, sr, -sc, sc], axis=-1)
    reps = LANES // ATT_QK_DIM
    return jnp.tile(cos, (1, reps)), jnp.tile(sin, (1, reps))


def _attn_kernel(lam_ref, q_ref, kt_ref, v_ref, g_ref, o_ref, m_ref, acc_ref, alpha_ref, s_ref, p_ref,
                 *, nk, tk, tq, rb, unroll, last_valid, post_scale):
    rows_all = 2 * tq

    def scores(j, slot):
        q = q_ref[0, 0].reshape(rows_all, LANES)
        s_ref[slot] = jnp.dot(q, kt_ref[0, 0, j], preferred_element_type=F32)

    def weighted_values(j, slot):
        v = v_ref[0, pl.ds(pl.multiple_of(j * tk, tk), tk), :]
        vo = jnp.concatenate([v, jnp.ones((tk, LANES), BF16)], axis=1)
        alpha = alpha_ref[slot]
        acc_ref[...] = (jnp.concatenate([alpha, alpha], axis=1) * acc_ref[...]
                        + jnp.dot(p_ref[slot], vo, preferred_element_type=F32))

    def softmax(slot, valid=tk):
        for r in range(rows_all // rb):
            rows = pl.ds(r * rb, rb)
            s = s_ref[slot, rows, :]
            if valid < tk:
                s = jnp.where(lax.broadcasted_iota(jnp.int32, s.shape, 1) < valid, s, -jnp.inf)
            m_old = m_ref[rows, :]
            m_new = jnp.maximum(m_old, jnp.max(s, axis=-1, keepdims=True))
            alpha_ref[slot, rows, :] = jnp.exp2(m_old - m_new)
            p_ref[slot, rows, :] = jnp.exp2(s - m_new[:, :1]).astype(BF16)
            m_ref[rows, :] = m_new

    def stage(j, slot, first=False, last=False):
        if not last:
            scores(j + 1, 1 - slot)
        if not first:
            weighted_values(j - 1, 1 - slot)
        softmax(slot, last_valid if last else tk)

    def group(jj, carry):
        for t in range(unroll):
            stage(unroll * jj + 1 + t, (1 + t) % 2)
        return carry

    m_ref[...] = jnp.full(m_ref.shape, -jnp.inf, F32)
    acc_ref[...] = jnp.zeros(acc_ref.shape, F32)
    scores(0, 0)
    stage(0, 0, first=True, last=nk == 1)
    ngroups = (nk - 2) // unroll if nk >= 2 else 0
    lax.fori_loop(0, ngroups, group, 0)
    for j in range(1 + unroll * ngroups, nk):
        stage(j, j % 2, last=j == nk - 1)
    weighted_values(nk - 1, (nk - 1) % 2)

    acc = acc_ref[...]
    o = acc[:, :LANES] / acc[:, LANES:]
    o = o[:tq] - lam_ref[0, 0] * o[tq:]
    o = o * lax.rsqrt(jnp.mean(o * o, axis=-1, keepdims=True) + EPS)
    o_ref[0] = (o * g_ref[...] * post_scale).astype(o_ref.dtype)


def diff_attention(q, kt, v, n_valid, lam, subln_g, post_scale, tq):
    b, h, _, s, _ = q.shape
    nk, tk = kt.shape[2], kt.shape[4]
    nkeys = v.shape[1]
    rows_all = 2 * tq
    last_valid = n_valid - (nk - 1) * tk
    assert nkeys == nk * tk and 0 < last_valid <= tk
    kern = functools.partial(_attn_kernel, nk=nk, tk=tk, tq=tq, rb=_tile(rows_all, 64), unroll=2, last_valid=last_valid,
                             post_scale=post_scale)
    return pl.pallas_call(
        kern,
        grid=(b, h, s // tq),
        in_specs=[
            pl.BlockSpec(memory_space=pltpu.SMEM),
            pl.BlockSpec((1, 1, 2, tq, LANES), lambda bi, hi, i: (bi, hi, 0, i, 0)),
            pl.BlockSpec((1, 1, nk, LANES, tk), lambda bi, hi, i: (bi, hi, 0, 0, 0)),
            pl.BlockSpec((1, nkeys, LANES), lambda bi, hi, i: (bi, 0, hi)),
            pl.BlockSpec((1, LANES), lambda bi, hi, i: (0, 0)),
        ],
        out_specs=pl.BlockSpec((1, tq, LANES), lambda bi, hi, i: (bi, i, hi)),
        out_shape=jax.ShapeDtypeStruct((b, s, h * LANES), BF16),
        scratch_shapes=[
            pltpu.VMEM((rows_all, LANES), F32),
            pltpu.VMEM((rows_all, 2 * LANES), F32),
            pltpu.VMEM((2, rows_all, LANES), F32),
            pltpu.VMEM((2, rows_all, tk), F32),
            pltpu.VMEM((2, rows_all, tk), BF16),
        ],
        compiler_params=_params("parallel", "parallel", "parallel"),
        name="diff_attention",
    )(lam.reshape(1, 1), q, kt, v, subln_g.reshape(1, LANES))


def keys_transposed(k, tk):
    b, nkeys, _ = k.shape
    k = k.reshape(b, nkeys // tk, tk, ATT_HEADS, LANES)
    return k.transpose(0, 3, 1, 4, 2)


def _sgu_kernel(u_ref, v_ref, g_ref, b_ref, ws_ref, bs_ref, o_ref):
    v = jax.nn.gelu(v_ref[0].astype(F32))
    mu = jnp.mean(v, axis=-1, keepdims=True)
    var = jnp.mean(jnp.square(v - mu), axis=-1, keepdims=True)
    v = ((v - mu) * lax.rsqrt(var + EPS) * g_ref[...] + b_ref[...]).astype(BF16)
    tn = v.shape[0]
    cw = SGU_WIDTH // SGU_GROUPS
    for c in range(tn // SGU_CHUNK):
        rows = slice(c * SGU_CHUNK, (c + 1) * SGU_CHUNK)
        for g in range(SGU_GROUPS):
            cols = slice(g * cw, (g + 1) * cw)
            mixed = jnp.dot(ws_ref[g], v[rows, cols], preferred_element_type=F32) + bs_ref[g]
            u = jax.nn.gelu(u_ref[0, rows, cols].astype(F32))
            o_ref[0, rows, cols] = (u * mixed).astype(o_ref.dtype)


def spatial_gating(proj, ln_g, ln_b, w_s, b_s):
    b, n, _ = proj.shape
    tn = _tile(n, 512)
    w = SGU_WIDTH
    u_blk = 3 * ATT_WIDTH // w
    return pl.pallas_call(
        _sgu_kernel,
        grid=(b, n // tn),
        in_specs=[
            pl.BlockSpec((1, tn, w), lambda bi, i: (bi, i, u_blk)),
            pl.BlockSpec((1, tn, w), lambda bi, i: (bi, i, u_blk + 1)),
            pl.BlockSpec((1, w), lambda bi, i: (0, 0)),
            pl.BlockSpec((1, w), lambda bi, i: (0, 0)),
            pl.BlockSpec((SGU_GROUPS, SGU_CHUNK, SGU_CHUNK), lambda bi, i: (0, 0, 0)),
            pl.BlockSpec((SGU_GROUPS, SGU_CHUNK, 1), lambda bi, i: (0, 0, 0)),
        ],
        out_specs=pl.BlockSpec((1, tn, w), lambda bi, i: (bi, i, 0)),
        out_shape=jax.ShapeDtypeStruct((b, n, w), BF16),
        compiler_params=_params("parallel", "parallel"),
        name="spatial_gating",
    )(proj, proj, ln_g.reshape(1, w), ln_b.reshape(1, w), w_s.astype(BF16),
      b_s.reshape(SGU_GROUPS, SGU_CHUNK, 1))


def _glu(a_ref, g_ref):
    return a_ref[0].astype(F32) * jax.nn.sigmoid(g_ref[0].astype(F32))


def _conv_kernel(a_ref, g_ref, ap_ref, gp_ref, an_ref, gn_ref, w_ref, b_ref, lg_ref, lb_ref, o_ref, h_ref, *, rc):
    i = pl.program_id(1)
    tn = a_ref.shape[1]
    h_ref[pl.ds(HALO, tn), :] = _glu(a_ref, g_ref)
    h_ref[pl.ds(0, HALO), :] = jnp.where(i > 0, _glu(ap_ref, gp_ref), 0.0)
    h_ref[pl.ds(HALO + tn, HALO), :] = jnp.where(i < pl.num_programs(1) - 1, _glu(an_ref, gn_ref), 0.0)
    first = HALO - CONV_K // 2
    wrows = rc + 2 * HALO
    sub = 8

    def chunk(r, carry):
        r0 = pl.multiple_of(r * rc, rc)
        cols = []
        for c in range(CONV_WIDTH // LANES):
            lanes = pl.ds(c * LANES, LANES)
            win = h_ref[pl.ds(r0, wrows), lanes]
            acc = jnp.zeros((rc, LANES), F32) + b_ref[:, lanes]
            for rot in range(sub):
                shifted = win if rot == 0 else pltpu.roll(win, wrows - rot, 0)
                for k in range(CONV_K):
                    off = first + k
                    if off % sub == rot:
                        base = off - rot
                        acc = acc + w_ref[pl.ds(k, 1), lanes] * shifted[base:base + rc]
            cols.append(acc)
        acc = jnp.concatenate(cols, axis=1)
        mu = jnp.mean(acc, axis=-1, keepdims=True)
        var = jnp.mean(jnp.square(acc - mu), axis=-1, keepdims=True)
        y = (acc - mu) * lax.rsqrt(var + EPS) * lg_ref[...] + lb_ref[...]
        o_ref[0, pl.ds(r0, rc), :] = (y * jax.nn.sigmoid(y)).astype(o_ref.dtype)
        return carry

    lax.fori_loop(0, tn // rc, chunk, 0)


def conformer_conv(proj, w_dw, b_dw, ln_g, ln_b):
    b, n, _ = proj.shape
    tn = _tile(n, 512)
    w = CONV_WIDTH
    a_blk = (3 * ATT_WIDTH + 2 * SGU_WIDTH) // w
    nh = n // HALO
    per = tn // HALO

    def main(c):
        return pl.BlockSpec((1, tn, w), lambda bi, i: (bi, i, c))

    def prev(c):
        return pl.BlockSpec((1, HALO, w), lambda bi, i: (bi, jnp.maximum(i * per - 1, 0), c))

    def nxt(c):
        return pl.BlockSpec((1, HALO, w), lambda bi, i: (bi, jnp.minimum((i + 1) * per, nh - 1), c))

    vec = pl.BlockSpec((1, w), lambda bi, i: (0, 0))
    return pl.pallas_call(
        functools.partial(_conv_kernel, rc=_tile(tn, 64)),
        grid=(b, n // tn),
        in_specs=[main(a_blk), main(a_blk + 1), prev(a_blk), prev(a_blk + 1), nxt(a_blk), nxt(a_blk + 1),
                  pl.BlockSpec((CONV_K, w), lambda bi, i: (0, 0)), vec, vec, vec],
        out_specs=pl.BlockSpec((1, tn, w), lambda bi, i: (bi, i, 0)),
        out_shape=jax.ShapeDtypeStruct((b, n, w), BF16),
        scratch_shapes=[pltpu.VMEM((tn + 2 * HALO, w), F32)],
        compiler_params=_params("parallel", "parallel"),
        name="conformer_conv",
    )(proj, proj, proj, proj, proj, proj, w_dw, b_dw.reshape(1, w), ln_g.reshape(1, w), ln_b.reshape(1, w))


def _merge_kernel(h_ref, att_ref, sgu_ref, conv_ref, wg0, wg1, wg2, bg0, bg1, bg2, wa, ws, wc, o_ref):
    h = h_ref[...]
    y = None
    for br_ref, wg, bg, wo in ((att_ref, wg0, bg0, wa), (sgu_ref, wg1, bg1, ws), (conv_ref, wg2, bg2, wc)):
        gate = jax.nn.sigmoid(jnp.dot(h, wg[...], preferred_element_type=F32) + bg[...])
        t = gate * jnp.dot(br_ref[...], wo[...], preferred_element_type=F32)
        y = t if y is None else y + t
    o_ref[...] = y.astype(o_ref.dtype)


def merge_branches(h, att, sgu, conv, w_gate, b_gate, w_att_out, w_sgu_out, w_conv_out):
    m, d = h.shape
    tm, tn = _tile(m, 1024), _tile(d, 512)
    nj = d // tn
    bw = att.shape[1]

    def wg(br):
        return pl.BlockSpec((d, tn), lambda i, j: (0, br * nj + j))

    def bg(br):
        return pl.BlockSpec((1, tn), lambda i, j: (0, br * nj + j))

    row = pl.BlockSpec((tm, bw), lambda i, j: (i, 0))
    wout = pl.BlockSpec((bw, tn), lambda i, j: (0, j))
    b_gate = b_gate.reshape(1, N_BRANCH * d)
    return pl.pallas_call(
        _merge_kernel,
        grid=(m // tm, nj),
        in_specs=[pl.BlockSpec((tm, d), lambda i, j: (i, 0)), row, row, row,
                  wg(0), wg(1), wg(2), bg(0), bg(1), bg(2), wout, wout, wout],
        out_specs=pl.BlockSpec((tm, tn), lambda i, j: (i, j)),
        out_shape=jax.ShapeDtypeStruct((m, d), BF16),
        compiler_params=_params("parallel", "parallel"),
        name="merge_branches",
    )(h, att, sgu, conv, w_gate, w_gate, w_gate, b_gate, b_gate, b_gate, w_att_out, w_sgu_out, w_conv_out)


def _proj_res_kernel(y_ref, w_ref, x_ref, g_ref, ng_ref, nsh_ref, nsc_ref, o_ref, hn_ref):
    x_new = x_ref[...] + g_ref[0] * jnp.dot(y_ref[...], w_ref[...], preferred_element_type=F32)
    _store_residual(x_new, o_ref, hn_ref, ng_ref, nsh_ref, nsc_ref)


def _next_norm_args(norm, d):
    g, shift, scale = norm
    return g.reshape(1, d), shift, scale


def proj_residual(y, w, x, gate, rows_per_batch, norm, h_dtype):
    m, k = y.shape
    d = w.shape[1]
    tm = _tile(rows_per_batch, 512)
    per = rows_per_batch // tm
    row = pl.BlockSpec((tm, d), lambda i: (i, 0))
    per_batch = pl.BlockSpec((1, 1, d), lambda i: (i // per, 0, 0))
    return pl.pallas_call(
        _proj_res_kernel,
        grid=(m // tm,),
        in_specs=[
            pl.BlockSpec((tm, k), lambda i: (i, 0)),
            pl.BlockSpec((k, d), lambda i: (0, 0)),
            row, per_batch,
            pl.BlockSpec((1, d), lambda i: (0, 0)), per_batch, per_batch,
        ],
        out_specs=[row, row],
        out_shape=[jax.ShapeDtypeStruct((m, d), F32), jax.ShapeDtypeStruct((m, d), h_dtype)],
        compiler_params=_params("parallel"),
        name="proj_residual",
    )(y, w, x, gate, *_next_norm_args(norm, d))


def _swiglu_out(h_ref, w1_ref, w3_ref, w2_ref, parts=1):
    rows = h_ref.shape[0] // parts
    outs = []
    for r in range(parts):
        h = h_ref[pl.ds(r * rows, rows), :]
        a = jnp.dot(h, w1_ref[0], preferred_element_type=F32)
        b = jnp.dot(h, w3_ref[0], preferred_element_type=F32)
        z = (a * jax.nn.sigmoid(a) * b).astype(BF16)
        outs.append(jnp.dot(z, w2_ref[0], preferred_element_type=F32))
    return jnp.concatenate(outs, axis=0)


def _ffn_kernel(h_ref, w1_ref, w3_ref, w2_ref, x_ref, g_ref, ng_ref, nsh_ref, nsc_ref, o_ref, hn_ref, acc_ref):
    f = pl.program_id(1)

    @pl.when(f == 0)
    def _():
        acc_ref[...] = jnp.zeros(acc_ref.shape, F32)

    acc_ref[...] += _swiglu_out(h_ref, w1_ref, w3_ref, w2_ref)

    @pl.when(f == pl.num_programs(1) - 1)
    def _():
        _store_residual(x_ref[...] + g_ref[0] * acc_ref[...], o_ref, hn_ref, ng_ref, nsh_ref, nsc_ref)


def ffn_residual(h, w1, w3, w2, x, gate, rows_per_batch, norm, h_dtype):
    m, d = h.shape
    fdim = w1.shape[2]
    tm = _tile(rows_per_batch, 512)
    tf = _tile(fdim, 512)
    per = rows_per_batch // tm
    row = pl.BlockSpec((tm, d), lambda i, f: (i, 0))
    per_batch = pl.BlockSpec((1, 1, d), lambda i, f: (i // per, 0, 0))
    return pl.pallas_call(
        _ffn_kernel,
        grid=(m // tm, fdim // tf),
        in_specs=[
            row,
            pl.BlockSpec((1, d, tf), lambda i, f: (0, 0, f)),
            pl.BlockSpec((1, d, tf), lambda i, f: (0, 0, f)),
            pl.BlockSpec((1, tf, d), lambda i, f: (0, f, 0)),
            row, per_batch,
            pl.BlockSpec((1, d), lambda i, f: (0, 0)), per_batch, per_batch,
        ],
        out_specs=[row, row],
        out_shape=[jax.ShapeDtypeStruct((m, d), F32), jax.ShapeDtypeStruct((m, d), h_dtype)],
        scratch_shapes=[pltpu.VMEM((tm, d), F32)],
        compiler_params=_params("parallel", "arbitrary"),
        name="ffn_residual",
    )(h, w1, w3, w2, x, gate, *_next_norm_args(norm, d))


def _router_kernel(h_ref, w_ref, b_ref, sel_ref, wts_ref):
    logits = jnp.dot(h_ref[...].astype(BF16), w_ref[...], preferred_element_type=F32) + b_ref[...]
    ne = logits.shape[-1]
    idx = lax.broadcasted_iota(jnp.int32, logits.shape, 1)
    v1 = jnp.max(logits, axis=-1, keepdims=True)
    i1 = jnp.min(jnp.where(logits == v1, idx, ne), axis=-1, keepdims=True)
    rest = jnp.where(idx == i1, -jnp.inf, logits)
    v2 = jnp.max(rest, axis=-1, keepdims=True)
    i2 = jnp.min(jnp.where(rest == v2, idx, ne), axis=-1, keepdims=True)
    e2 = jnp.exp(v2 - v1)
    sel_ref[...] = jnp.concatenate([i1, i2], axis=1)
    wts_ref[...] = jnp.concatenate([1.0 / (1.0 + e2), e2 / (1.0 + e2)], axis=1)


def router(h, w_r, b_r):
    m, d = h.shape
    ne = w_r.shape[1]
    tm = _tile(m, 512)
    return pl.pallas_call(
        _router_kernel,
        grid=(m // tm,),
        in_specs=[
            pl.BlockSpec((tm, d), lambda i: (i, 0)),
            pl.BlockSpec((d, ne), lambda i: (0, 0)),
            pl.BlockSpec((1, ne), lambda i: (0, 0)),
        ],
        out_specs=[pl.BlockSpec((tm, TOP_K), lambda i: (i, 0)), pl.BlockSpec((tm, TOP_K), lambda i: (i, 0))],
        out_shape=[jax.ShapeDtypeStruct((m, TOP_K), jnp.int32), jax.ShapeDtypeStruct((m, TOP_K), F32)],
        compiler_params=_params("parallel"),
        name="router",
    )(h, w_r.astype(BF16), b_r.reshape(1, ne))


def dispatch_plan(sel, tm):
    m = sel.shape[0]
    ne = N_EXPERTS
    npairs = m * TOP_K
    rows = npairs + ne * tm
    e_flat = sel.reshape(npairs)
    onehot = (e_flat[:, None] == jnp.arange(ne, dtype=jnp.int32)[None, :]).astype(jnp.int32)
    rank = jnp.cumsum(onehot, axis=0) - onehot
    counts = jnp.sum(onehot, axis=0)
    padded = (counts + tm - 1) // tm * tm
    ends = jnp.cumsum(padded)
    pos = (ends - padded)[e_flat] + jnp.sum(rank * onehot, axis=1)
    tile_start = jnp.arange(rows // tm, dtype=jnp.int32) * tm
    tile_expert = jnp.minimum(jnp.searchsorted(ends, tile_start, side="right"), ne - 1).astype(jnp.int32)
    meta = jnp.concatenate([tile_expert, (ends[-1:] // tm).astype(jnp.int32)])
    return pos.reshape(m, TOP_K), meta, rows


def _wait_rows(src_ref, dst_ref, sem, n):
    pltpu.make_async_copy(src_ref.at[pl.ds(0, n)], dst_ref.at[pl.ds(0, n)], sem).wait()


def _scatter_rows_kernel(pos_ref, h_ref, init_ref, o_ref, sem):
    del init_ref
    tm = h_ref.shape[0]

    def issue(r, carry):
        for k in range(TOP_K):
            pltpu.make_async_copy(h_ref.at[pl.ds(r, 1)], o_ref.at[pl.ds(pos_ref[0, k, r], 1)], sem).start()
        return carry

    lax.fori_loop(0, tm, issue, 0, unroll=8)
    for k in range(TOP_K):
        _wait_rows(h_ref, o_ref, sem, tm)


def scatter_rows(h, pos, rows, tm):
    m, d = h.shape
    pos_t = pos.reshape(m // tm, tm, TOP_K).transpose(0, 2, 1)
    return pl.pallas_call(
        _scatter_rows_kernel,
        grid=(m // tm,),
        in_specs=[
            pl.BlockSpec((1, TOP_K, tm), lambda i: (i, 0, 0), memory_space=pltpu.SMEM),
            pl.BlockSpec((tm, d), lambda i: (i, 0)),
            pl.BlockSpec(memory_space=pl.ANY),
        ],
        out_specs=pl.BlockSpec(memory_space=pl.ANY),
        out_shape=jax.ShapeDtypeStruct((rows, d), h.dtype),
        scratch_shapes=[pltpu.SemaphoreType.DMA(())],
        input_output_aliases={2: 0},
        compiler_params=_params("arbitrary"),
        name="scatter_rows",
    )(pos_t, h, jnp.zeros((rows, d), h.dtype))


def _expert_ffn_kernel(meta_ref, h_ref, w1_ref, w3_ref, w2_ref, o_ref, hb_ref):
    i, f = pl.program_id(0), pl.program_id(1)
    used = i < meta_ref[meta_ref.shape[0] - 1]

    @pl.when(f == 0)
    def _():
        hb_ref[...] = h_ref[...].astype(BF16)
        o_ref[...] = jnp.zeros(o_ref.shape, o_ref.dtype)

    @pl.when(used)
    def _():
        o_ref[...] += _swiglu_out(hb_ref, w1_ref, w3_ref, w2_ref)


def expert_ffn(hs, w1, w3, w2, meta, tm):
    r, d = hs.shape
    fdim = w1.shape[2]
    tf = _tile(fdim, 256)
    grid_spec = pltpu.PrefetchScalarGridSpec(
        num_scalar_prefetch=1,
        grid=(r // tm, fdim // tf),
        in_specs=[
            pl.BlockSpec((tm, d), lambda i, f, meta: (i, 0)),
            pl.BlockSpec((1, d, tf), lambda i, f, meta: (meta[i], 0, f)),
            pl.BlockSpec((1, d, tf), lambda i, f, meta: (meta[i], 0, f)),
            pl.BlockSpec((1, tf, d), lambda i, f, meta: (meta[i], f, 0)),
        ],
        out_specs=pl.BlockSpec((tm, d), lambda i, f, meta: (i, 0)),
        scratch_shapes=[pltpu.VMEM((tm, d), BF16)],
    )
    return pl.pallas_call(
        _expert_ffn_kernel,
        grid_spec=grid_spec,
        out_shape=jax.ShapeDtypeStruct((r, d), F32),
        compiler_params=_params("parallel", "arbitrary"),
        name="expert_ffn",
    )(meta, hs, w1, w3, w2)


def _combine_kernel(pos_ref, ys_ref, w_ref, x_ref, g_ref, ng_ref, nsh_ref, nsc_ref, o_ref, hn_ref, buf_ref, sem):
    tm = x_ref.shape[0]

    def issue(r, carry):
        for k in range(TOP_K):
            pltpu.make_async_copy(ys_ref.at[pl.ds(pos_ref[0, k, r], 1)], buf_ref.at[pl.ds(k * tm + r, 1)], sem).start()
        return carry

    lax.fori_loop(0, tm, issue, 0, unroll=8)
    _wait_rows(ys_ref, buf_ref, sem, TOP_K * tm)
    w = w_ref[...]
    y = w[:, 0:1] * buf_ref[pl.ds(0, tm), :]
    for k in range(1, TOP_K):
        y = y + w[:, k:k + 1] * buf_ref[pl.ds(k * tm, tm), :]
    _store_residual(x_ref[...] + g_ref[0] * y, o_ref, hn_ref, ng_ref, nsh_ref, nsc_ref)


def combine_residual(ys, pos, wts, x, gate, rows_per_batch, norm, h_dtype):
    m, d = x.shape
    tm = _tile(rows_per_batch, 256)
    per = rows_per_batch // tm
    pos_t = pos.reshape(m // tm, tm, TOP_K).transpose(0, 2, 1)
    row = pl.BlockSpec((tm, d), lambda i: (i, 0))
    per_batch = pl.BlockSpec((1, 1, d), lambda i: (i // per, 0, 0))
    return pl.pallas_call(
        _combine_kernel,
        grid=(m // tm,),
        in_specs=[
            pl.BlockSpec((1, TOP_K, tm), lambda i: (i, 0, 0), memory_space=pltpu.SMEM),
            pl.BlockSpec(memory_space=pl.ANY),
            pl.BlockSpec((tm, TOP_K), lambda i: (i, 0)),
            row, per_batch,
            pl.BlockSpec((1, d), lambda i: (0, 0)), per_batch, per_batch,
        ],
        out_specs=[row, row],
        out_shape=[jax.ShapeDtypeStruct((m, d), F32), jax.ShapeDtypeStruct((m, d), h_dtype)],
        scratch_shapes=[pltpu.VMEM((TOP_K * tm, d), F32), pltpu.SemaphoreType.DMA(())],
        compiler_params=_params("arbitrary"),
        name="combine_residual",
    )(pos_t, ys, wts, x, gate, *_next_norm_args(norm, d))


def moe_residual(h, w_r, b_r, w1, w3, w2, x, gate, rows_per_batch, norm, h_dtype):
    m = h.shape[0]
    tm = 1024 if m * TOP_K >= 8 * 1024 else 128
    sel, wts = router(h, w_r, b_r)
    pos, meta, rows = dispatch_plan(sel, tm)
    hs = scatter_rows(h, pos, rows, _tile(rows_per_batch, 512))
    ys = expert_ffn(hs, w1, w3, w2, meta, tm)
    return combine_residual(ys, pos, wts, x, gate, rows_per_batch, norm, h_dtype)


def _pad_keys(parts, tk):
    n = sum(t.shape[1] for t in parts)
    pad = -n % tk
    if pad:
        parts = parts + [jnp.zeros((parts[0].shape[0], pad, parts[0].shape[2]), parts[0].dtype)]
    return jnp.concatenate(parts, axis=1) if len(parts) > 1 else parts[0]


def _mixer(x, n, h, proj, keys, values, lam, lam_init, tq, tk, q, p):
    m, d = x.shape
    n_keys = sum(t.shape[1] for t in keys)
    att = diff_attention(q, keys_transposed(_pad_keys(keys, tk), tk), _pad_keys(values, tk), n_keys, lam,
                         p["subln_g"], 1 - lam_init, tq)
    sgu = spatial_gating(proj, p["sgu_ln_g"], p["sgu_ln_b"], p["w_spatial"], p["b_spatial"])
    conv = conformer_conv(proj, p["conv_w"], p["conv_b"], p["conv_ln_g"], p["conv_ln_b"])
    y = merge_branches(h, att.reshape(m, -1), sgu.reshape(m, -1), conv.reshape(m, -1),
                       p["w_gate"], p["b_gate"], p["w_att_out"], p["w_sgu_out"], p["w_conv_out"])
    norm2 = (p["norm2_g"], p["sh2"], p["sc2"])
    return proj_residual(y, p["w_o"], x, p["g1"], n, norm2, F32 if p["moe"] else BF16)


def _channel(x, h, n, p, next_norm, next_dtype):
    if p["moe"]:
        return moe_residual(h, p["router_w"], p["router_b"], p["w1"], p["w3"], p["w2"], x, p["g2"], n,
                            next_norm, next_dtype)
    return ffn_residual(h, p["w1"], p["w3"], p["w2"], x, p["g2"], n, next_norm, next_dtype)


def kernel(x, c, ctx, c_ctx, w_mod, b_mod, norm1_g, norm2_g, w_in, lam_q1, lam_k1, lam_q2, lam_k2, subln_g, w_att_out, sgu_ln_g, sgu_ln_b, w_spatial, b_spatial, w_sgu_out, conv_w, conv_b, conv_ln_g, conv_ln_b, w_conv_out, w_gate, b_gate, w_o, ffn_w1, ffn_w3, ffn_w2, router_w, router_b, moe_w1, moe_w3, moe_w2, final_g):
    b, s, d = x.shape
    nc = ctx.shape[1]
    depth = w_in.shape[0]
    cos, sin = rope_tables(s)
    ones, zeros = jnp.ones((nc, LANES), F32), jnp.zeros((nc, LANES), F32)

    c_rows = jnp.concatenate([c, c_ctx[None, :], jnp.zeros((SUBLANES - b - 1, d), F32)], axis=0)
    mods = modulation(c_rows, w_mod, b_mod)
    lats = [[t[:, None, :] for t in jnp.split(mods[i, :b], 6, axis=-1)] for i in range(depth)]
    cxs = [[jnp.broadcast_to(t[:, None, :], (b, 1, d)) for t in jnp.split(mods[i, b:b + 1], 6, axis=-1)]
           for i in range(depth)]
    zero = jnp.zeros((b, 1, d), F32)

    x = x.reshape(b * s, d)
    xc = ctx.reshape(b * nc, d)
    h_lat = normmod(x.reshape(b, s, d), norm1_g[0], lats[0][0], lats[0][1], BF16).reshape(b * s, d)
    h_ctx = normmod(ctx, norm1_g[0], cxs[0][0], cxs[0][1], BF16).reshape(b * nc, d)
    for i in range(depth):
        last = i == depth - 1
        lat, cx = lats[i], cxs[i]
        lam_init = 0.8 - 0.6 * math.exp(-0.3 * i)
        lam = (jnp.exp(jnp.sum(lam_q1[i] * lam_k1[i]).astype(F32))
               - jnp.exp(jnp.sum(lam_q2[i] * lam_k2[i]).astype(F32)) + lam_init)
        shared = dict(
            norm2_g=norm2_g[i], subln_g=subln_g[i], sgu_ln_g=sgu_ln_g[i], sgu_ln_b=sgu_ln_b[i],
            w_spatial=w_spatial[i], b_spatial=b_spatial[i], conv_w=conv_w[i], conv_b=conv_b[i],
            conv_ln_g=conv_ln_g[i], conv_ln_b=conv_ln_b[i],
            w_gate=w_gate[i].astype(BF16), b_gate=b_gate[i], w_att_out=w_att_out[i].astype(BF16),
            w_sgu_out=w_sgu_out[i].astype(BF16), w_conv_out=w_conv_out[i].astype(BF16), w_o=w_o[i].astype(BF16),
            moe=i % 2 == 1)
        j = i // 2
        if i % 2 == 0:
            shared.update(w1=ffn_w1[j][None].astype(BF16), w3=ffn_w3[j][None].astype(BF16),
                          w2=ffn_w2[j][None].astype(BF16))
        else:
            shared.update(w1=moe_w1[j].astype(BF16), w3=moe_w3[j].astype(BF16), w2=moe_w2[j].astype(BF16),
                          router_w=router_w[j], router_b=router_b[j])
        p_lat = dict(shared, g1=lat[2], sh2=lat[3], sc2=lat[4], g2=lat[5])
        p_ctx = dict(shared, g1=cx[2], sh2=cx[3], sc2=cx[4], g2=cx[5])
        w_in_i = w_in[i].astype(BF16)

        proj_l = matmul(h_lat, w_in_i, BF16).reshape(b, s, IN_COLS)
        proj_c = matmul(h_ctx, w_in_i, BF16).reshape(b, nc, IN_COLS)
        rope_l = rope_qk(proj_l, cos, sin)
        rope_c = rope_qk(proj_c, ones, zeros)
        v_l = proj_l[..., 2 * ATT_WIDTH:3 * ATT_WIDTH]
        v_c = proj_c[..., 2 * ATT_WIDTH:3 * ATT_WIDTH]
        x, h2_lat = _mixer(x, s, h_lat, proj_l, [rope_l[1], rope_c[1]], [v_l, v_c], lam, lam_init,
                           _tile(s, 512), ATT_KEY_TILE, rope_l[0], p_lat)
        if not last:
            xc, h2_ctx = _mixer(xc, nc, h_ctx, proj_c, [rope_c[1]], [v_c], lam, lam_init, nc, nc, rope_c[0], p_ctx)

        if last:
            _, out = _channel(x, h2_lat, s, p_lat, (final_g, zero, zero), F32)
            return out.reshape(b, s, d)
        x, h_lat = _channel(x, h2_lat, s, p_lat, (norm1_g[i + 1], lats[i + 1][0], lats[i + 1][1]), BF16)
        xc, h_ctx = _channel(xc, h2_ctx, nc, p_ctx, (norm1_g[i + 1], cxs[i + 1][0], cxs[i + 1][1]), BF16)
```

```python
import functools
import math

import jax
import jax.numpy as jnp
from jax import lax
from jax.experimental import pallas as pl
from jax.experimental.pallas import tpu as pltpu

D_MODEL = 2048
BATCH = 2
SEQ = 16384
DEPTH = 4
GRID_W = 64
CTX_LEN = 256
ATT_HEADS = 8
ATT_QK_DIM = 64
ATT_V_DIM = 2 * ATT_QK_DIM
ATT_WIDTH = ATT_HEADS * ATT_V_DIM
ROPE_BASE = 10000.0
SGU_CHUNK = 128
SGU_GROUPS = 8
SGU_WIDTH = 1024
CONV_WIDTH = 1024
CONV_K = 31
IN_COLS = 3 * ATT_WIDTH + 2 * SGU_WIDTH + 2 * CONV_WIDTH
N_BRANCH = 3
FFN_DIM = 5632
N_EXPERTS = 8
TOP_K = 2
EXPERT_DIM = 2816
EPS = 1e-6

LANES = 128
SUBLANES = 8
ATT_KEY_TILE = 640
HALO = 16
VMEM_LIMIT = 56 * 1024 * 1024
ROW_TILE = 512
PROJ_ROW_TILE = 2048
MERGE_ROW_TILE = 1024
COL_TILE = 1024
MERGE_COL_TILE = 512
FFN_COL_TILE = 512
EXPERT_COL_TILE = 256
EXPERT_ROW_TILE = 1024
SMALL_EXPERT_ROW_TILE = 128
COMBINE_ROW_TILE = 256
ATT_QUERY_TILE = 512
ATT_SOFTMAX_ROWS = 64
CONV_ROWS = 64
BF16 = jnp.bfloat16
F32 = jnp.float32


def _params(*sem):
    return pltpu.CompilerParams(dimension_semantics=sem, vmem_limit_bytes=VMEM_LIMIT)


def _tile(n, pref):
    t = min(pref, n)
    while n % t:
        t //= 2
    return t


def _rownorm_mod(x, g, shift, scale):
    y = x * lax.rsqrt(jnp.mean(x * x, axis=-1, keepdims=True) + EPS)
    return y * g * (1 + scale) + shift


def _normmod_kernel(x_ref, g_ref, sh_ref, sc_ref, o_ref):
    o_ref[0] = _rownorm_mod(x_ref[0], g_ref[...], sh_ref[0], sc_ref[0]).astype(o_ref.dtype)


def _store_residual(x_new, o_ref, hn_ref, ng_ref, nsh_ref, nsc_ref):
    o_ref[...] = x_new
    hn_ref[...] = _rownorm_mod(x_new, ng_ref[...], nsh_ref[0], nsc_ref[0]).astype(hn_ref.dtype)


def normmod(x, g, shift, scale, out_dtype):
    b, n, d = x.shape
    tn = _tile(n, ROW_TILE)
    return pl.pallas_call(
        _normmod_kernel,
        grid=(b, n // tn),
        in_specs=[
            pl.BlockSpec((1, tn, d), lambda bi, i: (bi, i, 0)),
            pl.BlockSpec((1, d), lambda bi, i: (0, 0)),
            pl.BlockSpec((1, 1, d), lambda bi, i: (bi, 0, 0)),
            pl.BlockSpec((1, 1, d), lambda bi, i: (bi, 0, 0)),
        ],
        out_specs=pl.BlockSpec((1, tn, d), lambda bi, i: (bi, i, 0)),
        out_shape=jax.ShapeDtypeStruct((b, n, d), out_dtype),
        compiler_params=_params("parallel", "parallel"),
        name="normmod",
    )(x, g.reshape(1, d), shift, scale)


def _mod_kernel(c_ref, w_ref, b_ref, o_ref):
    c = c_ref[...]
    a = (c * jax.nn.sigmoid(c)).astype(BF16)
    o_ref[0] = jnp.dot(a, w_ref[0].astype(BF16), preferred_element_type=F32) + b_ref[0]


def modulation(c_rows, w_mod, b_mod):
    r, d = c_rows.shape
    nl, _, n = w_mod.shape
    tn = _tile(n, COL_TILE)
    return pl.pallas_call(
        _mod_kernel,
        grid=(nl, n // tn),
        in_specs=[
            pl.BlockSpec((r, d), lambda l, j: (0, 0)),
            pl.BlockSpec((1, d, tn), lambda l, j: (l, 0, j)),
            pl.BlockSpec((1, 1, tn), lambda l, j: (l, 0, j)),
        ],
        out_specs=pl.BlockSpec((1, r, tn), lambda l, j: (l, 0, j)),
        out_shape=jax.ShapeDtypeStruct((nl, r, n), F32),
        compiler_params=_params("parallel", "parallel"),
        name="modulation",
    )(c_rows, w_mod, b_mod.reshape(nl, 1, n))


def _mm_kernel(a_ref, b_ref, o_ref):
    o_ref[...] = jnp.dot(a_ref[...], b_ref[...], preferred_element_type=F32).astype(o_ref.dtype)


def matmul(a, b, out_dtype):
    m, k = a.shape
    n = b.shape[1]
    tm, tn = _tile(m, PROJ_ROW_TILE), _tile(n, COL_TILE)
    return pl.pallas_call(
        _mm_kernel,
        grid=(m // tm, n // tn),
        in_specs=[
            pl.BlockSpec((tm, k), lambda i, j: (i, 0)),
            pl.BlockSpec((k, tn), lambda i, j: (0, j)),
        ],
        out_specs=pl.BlockSpec((tm, tn), lambda i, j: (i, j)),
        out_shape=jax.ShapeDtypeStruct((m, n), out_dtype),
        compiler_params=_params("parallel", "parallel"),
        name="matmul",
    )(a, b)


def _swap_halves(x):
    n = x.shape[-1]
    quarter = ATT_QK_DIM // 4
    lane = lax.broadcasted_iota(jnp.int32, x.shape, x.ndim - 1)
    up = pltpu.roll(x, n - quarter, x.ndim - 1)
    down = pltpu.roll(x, quarter, x.ndim - 1)
    return jnp.where(lane % (2 * quarter) < quarter, up, down)


def _rope_kernel(q_ref, k_ref, cos_ref, sin_ref, qo_ref, ko_ref):
    cos = cos_ref[...]
    sin = sin_ref[...]
    scale = ATT_QK_DIM ** -0.5 * math.log2(math.e)
    lane = lax.broadcasted_iota(jnp.int32, cos.shape, 1)
    first = lane < ATT_QK_DIM
    for h in range(ATT_HEADS):
        cols = slice(h * LANES, (h + 1) * LANES)
        q = q_ref[0, :, cols].astype(F32)
        k = k_ref[0, :, cols].astype(F32)
        qr = (q * cos + _swap_halves(q) * sin) * scale
        kr = k * cos + _swap_halves(k) * sin
        qo_ref[0, h, 0] = jnp.where(first, qr, 0.0).astype(qo_ref.dtype)
        qo_ref[0, h, 1] = jnp.where(first, 0.0, qr).astype(qo_ref.dtype)
        ko_ref[0, :, cols] = kr.astype(ko_ref.dtype)


def rope_qk(proj, cos, sin):
    b, n, _ = proj.shape
    tn = _tile(n, ROW_TILE)
    w = ATT_WIDTH
    return pl.pallas_call(
        _rope_kernel,
        grid=(b, n // tn),
        in_specs=[
            pl.BlockSpec((1, tn, w), lambda bi, i: (bi, i, 0)),
            pl.BlockSpec((1, tn, w), lambda bi, i: (bi, i, 1)),
            pl.BlockSpec((tn, LANES), lambda bi, i: (i, 0)),
            pl.BlockSpec((tn, LANES), lambda bi, i: (i, 0)),
        ],
        out_specs=[
            pl.BlockSpec((1, ATT_HEADS, 2, tn, LANES), lambda bi, i: (bi, 0, 0, i, 0)),
            pl.BlockSpec((1, tn, w), lambda bi, i: (bi, i, 0)),
        ],
        out_shape=[
            jax.ShapeDtypeStruct((b, ATT_HEADS, 2, n, LANES), BF16),
            jax.ShapeDtypeStruct((b, n, w), BF16),
        ],
        compiler_params=_params("parallel", "parallel"),
        name="rope_qk",
    )(proj, proj, cos, sin)


def rope_tables(n):
    rows = n // GRID_W
    row = jnp.broadcast_to(jnp.arange(rows, dtype=jnp.int32)[:, None], (rows, GRID_W)).reshape(-1)
    col = jnp.broadcast_to(jnp.arange(GRID_W, dtype=jnp.int32)[None, :], (rows, GRID_W)).reshape(-1)
    half = ATT_QK_DIM // 2
    inv = ROPE_BASE ** (-jnp.arange(0, half, 2, dtype=F32) / half)
    ang_r = row.astype(F32)[:, None] * inv
    ang_c = col.astype(F32)[:, None] * inv
    cr, sr, cc, sc = jnp.cos(ang_r), jnp.sin(ang_r), jnp.cos(ang_c), jnp.sin(ang_c)
    cos = jnp.concatenate([cr, cr, cc, cc], axis=-1)
    sin = jnp.concatenate([-sr, sr, -sc, sc], axis=-1)
    reps = LANES // ATT_QK_DIM
    return jnp.tile(cos, (1, reps)), jnp.tile(sin, (1, reps))


def _attn_kernel(lam_ref, q_ref, kt_ref, v_ref, g_ref, o_ref, m_ref, acc_ref, alpha_ref, s_ref, p_ref,
                 *, nk, tk, tq, rb, unroll, last_valid, post_scale):
    rows_all = 2 * tq

    def scores(j, slot):
        q = q_ref[0, 0].reshape(rows_all, LANES)
        s_ref[slot] = jnp.dot(q, kt_ref[0, 0, j], preferred_element_type=F32)

    def weighted_values(j, slot):
        v = v_ref[0, pl.ds(pl.multiple_of(j * tk, tk), tk), :]
        vo = jnp.concatenate([v, jnp.ones((tk, LANES), BF16)], axis=1)
        alpha = alpha_ref[slot]
        acc_ref[...] = (jnp.concatenate([alpha, alpha], axis=1) * acc_ref[...]
                        + jnp.dot(p_ref[slot], vo, preferred_element_type=F32))

    def softmax(slot, valid=tk):
        for r in range(rows_all // rb):
            rows = pl.ds(r * rb, rb)
            s = s_ref[slot, rows, :]
            if valid < tk:
                s = jnp.where(lax.broadcasted_iota(jnp.int32, s.shape, 1) < valid, s, -jnp.inf)
            m_old = m_ref[rows, :]
            m_new = jnp.maximum(m_old, jnp.max(s, axis=-1, keepdims=True))
            alpha_ref[slot, rows, :] = jnp.exp2(m_old - m_new)
            p_ref[slot, rows, :] = jnp.exp2(s - m_new[:, :1]).astype(BF16)
            m_ref[rows, :] = m_new

    def stage(j, slot, first=False, last=False):
        if not last:
            scores(j + 1, 1 - slot)
        if not first:
            weighted_values(j - 1, 1 - slot)
        softmax(slot, last_valid if last else tk)

    def group(jj, carry):
        for t in range(unroll):
            stage(unroll * jj + 1 + t, (1 + t) % 2)
        return carry

    m_ref[...] = jnp.full(m_ref.shape, -jnp.inf, F32)
    acc_ref[...] = jnp.zeros(acc_ref.shape, F32)
    scores(0, 0)
    stage(0, 0, first=True, last=nk == 1)
    ngroups = (nk - 2) // unroll if nk >= 2 else 0
    lax.fori_loop(0, ngroups, group, 0)
    for j in range(1 + unroll * ngroups, nk):
        stage(j, j % 2, last=j == nk - 1)
    weighted_values(nk - 1, (nk - 1) % 2)

    acc = acc_ref[...]
    o = acc[:, :LANES] / acc[:, LANES:]
    o = o[:tq] - lam_ref[0, 0] * o[tq:]
    o = o * lax.rsqrt(jnp.mean(o * o, axis=-1, keepdims=True) + EPS)
    o_ref[0] = (o * g_ref[...] * post_scale).astype(o_ref.dtype)


def diff_attention(q, kt, v, n_valid, lam, subln_g, post_scale, tq):
    b, h, _, s, _ = q.shape
    nk, tk = kt.shape[2], kt.shape[4]
    nkeys = v.shape[1]
    rows_all = 2 * tq
    last_valid = n_valid - (nk - 1) * tk
    assert nkeys == nk * tk and 0 < last_valid <= tk
    kern = functools.partial(_attn_kernel, nk=nk, tk=tk, tq=tq, rb=_tile(rows_all, ATT_SOFTMAX_ROWS), unroll=2,
                             last_valid=last_valid,
                             post_scale=post_scale)
    return pl.pallas_call(
        kern,
        grid=(b, h, s // tq),
        in_specs=[
            pl.BlockSpec(memory_space=pltpu.SMEM),
            pl.BlockSpec((1, 1, 2, tq, LANES), lambda bi, hi, i: (bi, hi, 0, i, 0)),
            pl.BlockSpec((1, 1, nk, LANES, tk), lambda bi, hi, i: (bi, hi, 0, 0, 0)),
            pl.BlockSpec((1, nkeys, LANES), lambda bi, hi, i: (bi, 0, hi)),
            pl.BlockSpec((1, LANES), lambda bi, hi, i: (0, 0)),
        ],
        out_specs=pl.BlockSpec((1, tq, LANES), lambda bi, hi, i: (bi, i, hi)),
        out_shape=jax.ShapeDtypeStruct((b, s, h * LANES), BF16),
        scratch_shapes=[
            pltpu.VMEM((rows_all, LANES), F32),
            pltpu.VMEM((rows_all, 2 * LANES), F32),
            pltpu.VMEM((2, rows_all, LANES), F32),
            pltpu.VMEM((2, rows_all, tk), F32),
            pltpu.VMEM((2, rows_all, tk), BF16),
        ],
        compiler_params=_params("parallel", "parallel", "parallel"),
        name="diff_attention",
    )(lam.reshape(1, 1), q, kt, v, subln_g.reshape(1, LANES))


def keys_transposed(k, tk):
    b, nkeys, _ = k.shape
    k = k.reshape(b, nkeys // tk, tk, ATT_HEADS, LANES)
    return k.transpose(0, 3, 1, 4, 2)


def _sgu_kernel(u_ref, v_ref, g_ref, b_ref, ws_ref, bs_ref, o_ref):
    v = jax.nn.gelu(v_ref[0].astype(F32))
    mu = jnp.mean(v, axis=-1, keepdims=True)
    var = jnp.mean(jnp.square(v - mu), axis=-1, keepdims=True)
    v = ((v - mu) * lax.rsqrt(var + EPS) * g_ref[...] + b_ref[...]).astype(BF16)
    tn = v.shape[0]
    cw = SGU_WIDTH // SGU_GROUPS
    for c in range(tn // SGU_CHUNK):
        rows = slice(c * SGU_CHUNK, (c + 1) * SGU_CHUNK)
        for g in range(SGU_GROUPS):
            cols = slice(g * cw, (g + 1) * cw)
            mixed = jnp.dot(ws_ref[g], v[rows, cols], preferred_element_type=F32) + bs_ref[g]
            u = jax.nn.gelu(u_ref[0, rows, cols].astype(F32))
            o_ref[0, rows, cols] = (u * mixed).astype(o_ref.dtype)


def spatial_gating(proj, ln_g, ln_b, w_s, b_s):
    b, n, _ = proj.shape
    tn = _tile(n, ROW_TILE)
    w = SGU_WIDTH
    u_blk = 3 * ATT_WIDTH // w
    return pl.pallas_call(
        _sgu_kernel,
        grid=(b, n // tn),
        in_specs=[
            pl.BlockSpec((1, tn, w), lambda bi, i: (bi, i, u_blk)),
            pl.BlockSpec((1, tn, w), lambda bi, i: (bi, i, u_blk + 1)),
            pl.BlockSpec((1, w), lambda bi, i: (0, 0)),
            pl.BlockSpec((1, w), lambda bi, i: (0, 0)),
            pl.BlockSpec((SGU_GROUPS, SGU_CHUNK, SGU_CHUNK), lambda bi, i: (0, 0, 0)),
            pl.BlockSpec((SGU_GROUPS, SGU_CHUNK, 1), lambda bi, i: (0, 0, 0)),
        ],
        out_specs=pl.BlockSpec((1, tn, w), lambda bi, i: (bi, i, 0)),
        out_shape=jax.ShapeDtypeStruct((b, n, w), BF16),
        compiler_params=_params("parallel", "parallel"),
        name="spatial_gating",
    )(proj, proj, ln_g.reshape(1, w), ln_b.reshape(1, w), w_s.astype(BF16),
      b_s.reshape(SGU_GROUPS, SGU_CHUNK, 1))


def _glu(a_ref, g_ref):
    return a_ref[0].astype(F32) * jax.nn.sigmoid(g_ref[0].astype(F32))


def _conv_kernel(a_ref, g_ref, ap_ref, gp_ref, an_ref, gn_ref, w_ref, b_ref, lg_ref, lb_ref, o_ref, h_ref, *, rc):
    i = pl.program_id(1)
    tn = a_ref.shape[1]
    h_ref[pl.ds(HALO, tn), :] = _glu(a_ref, g_ref)
    h_ref[pl.ds(0, HALO), :] = jnp.where(i > 0, _glu(ap_ref, gp_ref), 0.0)
    h_ref[pl.ds(HALO + tn, HALO), :] = jnp.where(i < pl.num_programs(1) - 1, _glu(an_ref, gn_ref), 0.0)
    first = HALO - CONV_K // 2
    wrows = rc + 2 * HALO
    sub = 8

    def chunk(r, carry):
        r0 = pl.multiple_of(r * rc, rc)
        cols = []
        for c in range(CONV_WIDTH // LANES):
            lanes = pl.ds(c * LANES, LANES)
            win = h_ref[pl.ds(r0, wrows), lanes]
            acc = jnp.zeros((rc, LANES), F32) + b_ref[:, lanes]
            for rot in range(sub):
                shifted = win if rot == 0 else pltpu.roll(win, wrows - rot, 0)
                for k in range(CONV_K):
                    off = first + k
                    if off % sub == rot:
                        base = off - rot
                        acc = acc + w_ref[pl.ds(k, 1), lanes] * shifted[base:base + rc]
            cols.append(acc)
        acc = jnp.concatenate(cols, axis=1)
        mu = jnp.mean(acc, axis=-1, keepdims=True)
        var = jnp.mean(jnp.square(acc - mu), axis=-1, keepdims=True)
        y = (acc - mu) * lax.rsqrt(var + EPS) * lg_ref[...] + lb_ref[...]
        o_ref[0, pl.ds(r0, rc), :] = (y * jax.nn.sigmoid(y)).astype(o_ref.dtype)
        return carry

    lax.fori_loop(0, tn // rc, chunk, 0)


def conformer_conv(proj, w_dw, b_dw, ln_g, ln_b):
    b, n, _ = proj.shape
    tn = _tile(n, ROW_TILE)
    w = CONV_WIDTH
    a_blk = (3 * ATT_WIDTH + 2 * SGU_WIDTH) // w
    nh = n // HALO
    per = tn // HALO

    def main(c):
        return pl.BlockSpec((1, tn, w), lambda bi, i: (bi, i, c))

    def prev(c):
        return pl.BlockSpec((1, HALO, w), lambda bi, i: (bi, jnp.maximum(i * per - 1, 0), c))

    def nxt(c):
        return pl.BlockSpec((1, HALO, w), lambda bi, i: (bi, jnp.minimum((i + 1) * per, nh - 1), c))

    vec = pl.BlockSpec((1, w), lambda bi, i: (0, 0))
    return pl.pallas_call(
        functools.partial(_conv_kernel, rc=_tile(tn, CONV_ROWS)),
        grid=(b, n // tn),
        in_specs=[main(a_blk), main(a_blk + 1), prev(a_blk), prev(a_blk + 1), nxt(a_blk), nxt(a_blk + 1),
                  pl.BlockSpec((CONV_K, w), lambda bi, i: (0, 0)), vec, vec, vec],
        out_specs=pl.BlockSpec((1, tn, w), lambda bi, i: (bi, i, 0)),
        out_shape=jax.ShapeDtypeStruct((b, n, w), BF16),
        scratch_shapes=[pltpu.VMEM((tn + 2 * HALO, w), F32)],
        compiler_params=_params("parallel", "parallel"),
        name="conformer_conv",
    )(proj, proj, proj, proj, proj, proj, w_dw, b_dw.reshape(1, w), ln_g.reshape(1, w), ln_b.reshape(1, w))


def _merge_kernel(h_ref, att_ref, sgu_ref, conv_ref, wg0, wg1, wg2, bg0, bg1, bg2, wa, ws, wc, o_ref):
    h = h_ref[...]
    y = None
    for br_ref, wg, bg, wo in ((att_ref, wg0, bg0, wa), (sgu_ref, wg1, bg1, ws), (conv_ref, wg2, bg2, wc)):
        gate = jax.nn.sigmoid(jnp.dot(h, wg[...], preferred_element_type=F32) + bg[...])
        t = gate * jnp.dot(br_ref[...], wo[...], preferred_element_type=F32)
        y = t if y is None else y + t
    o_ref[...] = y.astype(o_ref.dtype)


def merge_branches(h, att, sgu, conv, w_gate, b_gate, w_att_out, w_sgu_out, w_conv_out):
    m, d = h.shape
    tm, tn = _tile(m, MERGE_ROW_TILE), _tile(d, MERGE_COL_TILE)
    nj = d // tn
    bw = att.shape[1]

    def wg(br):
        return pl.BlockSpec((d, tn), lambda i, j: (0, br * nj + j))

    def bg(br):
        return pl.BlockSpec((1, tn), lambda i, j: (0, br * nj + j))

    row = pl.BlockSpec((tm, bw), lambda i, j: (i, 0))
    wout = pl.BlockSpec((bw, tn), lambda i, j: (0, j))
    b_gate = b_gate.reshape(1, N_BRANCH * d)
    return pl.pallas_call(
        _merge_kernel,
        grid=(m // tm, nj),
        in_specs=[pl.BlockSpec((tm, d), lambda i, j: (i, 0)), row, row, row,
                  wg(0), wg(1), wg(2), bg(0), bg(1), bg(2), wout, wout, wout],
        out_specs=pl.BlockSpec((tm, tn), lambda i, j: (i, j)),
        out_shape=jax.ShapeDtypeStruct((m, d), BF16),
        compiler_params=_params("parallel", "parallel"),
        name="merge_branches",
    )(h, att, sgu, conv, w_gate, w_gate, w_gate, b_gate, b_gate, b_gate, w_att_out, w_sgu_out, w_conv_out)


def _proj_res_kernel(y_ref, w_ref, x_ref, g_ref, ng_ref, nsh_ref, nsc_ref, o_ref, hn_ref):
    x_new = x_ref[...] + g_ref[0] * jnp.dot(y_ref[...], w_ref[...], preferred_element_type=F32)
    _store_residual(x_new, o_ref, hn_ref, ng_ref, nsh_ref, nsc_ref)


def _next_norm_args(norm, d):
    g, shift, scale = norm
    return g.reshape(1, d), shift, scale


def proj_residual(y, w, x, gate, rows_per_batch, norm, h_dtype):
    m, k = y.shape
    d = w.shape[1]
    tm = _tile(rows_per_batch, ROW_TILE)
    per = rows_per_batch // tm
    row = pl.BlockSpec((tm, d), lambda i: (i, 0))
    per_batch = pl.BlockSpec((1, 1, d), lambda i: (i // per, 0, 0))
    return pl.pallas_call(
        _proj_res_kernel,
        grid=(m // tm,),
        in_specs=[
            pl.BlockSpec((tm, k), lambda i: (i, 0)),
            pl.BlockSpec((k, d), lambda i: (0, 0)),
            row, per_batch,
            pl.BlockSpec((1, d), lambda i: (0, 0)), per_batch, per_batch,
        ],
        out_specs=[row, row],
        out_shape=[jax.ShapeDtypeStruct((m, d), F32), jax.ShapeDtypeStruct((m, d), h_dtype)],
        compiler_params=_params("parallel"),
        name="proj_residual",
    )(y, w, x, gate, *_next_norm_args(norm, d))


def _swiglu_out(h_ref, w1_ref, w3_ref, w2_ref, parts=1):
    rows = h_ref.shape[0] // parts
    outs = []
    for r in range(parts):
        h = h_ref[pl.ds(r * rows, rows), :]
        a = jnp.dot(h, w1_ref[0], preferred_element_type=F32)
        b = jnp.dot(h, w3_ref[0], preferred_element_type=F32)
        z = (a * jax.nn.sigmoid(a) * b).astype(BF16)
        outs.append(jnp.dot(z, w2_ref[0], preferred_element_type=F32))
    return jnp.concatenate(outs, axis=0)


def _ffn_kernel(h_ref, w1_ref, w3_ref, w2_ref, x_ref, g_ref, ng_ref, nsh_ref, nsc_ref, o_ref, hn_ref, acc_ref):
    f = pl.program_id(1)

    @pl.when(f == 0)
    def _():
        acc_ref[...] = jnp.zeros(acc_ref.shape, F32)

    acc_ref[...] += _swiglu_out(h_ref, w1_ref, w3_ref, w2_ref)

    @pl.when(f == pl.num_programs(1) - 1)
    def _():
        _store_residual(x_ref[...] + g_ref[0] * acc_ref[...], o_ref, hn_ref, ng_ref, nsh_ref, nsc_ref)


def ffn_residual(h, w1, w3, w2, x, gate, rows_per_batch, norm, h_dtype):
    m, d = h.shape
    fdim = w1.shape[2]
    tm = _tile(rows_per_batch, ROW_TILE)
    tf = _tile(fdim, FFN_COL_TILE)
    per = rows_per_batch // tm
    row = pl.BlockSpec((tm, d), lambda i, f: (i, 0))
    per_batch = pl.BlockSpec((1, 1, d), lambda i, f: (i // per, 0, 0))
    return pl.pallas_call(
        _ffn_kernel,
        grid=(m // tm, fdim // tf),
        in_specs=[
            row,
            pl.BlockSpec((1, d, tf), lambda i, f: (0, 0, f)),
            pl.BlockSpec((1, d, tf), lambda i, f: (0, 0, f)),
            pl.BlockSpec((1, tf, d), lambda i, f: (0, f, 0)),
            row, per_batch,
            pl.BlockSpec((1, d), lambda i, f: (0, 0)), per_batch, per_batch,
        ],
        out_specs=[row, row],
        out_shape=[jax.ShapeDtypeStruct((m, d), F32), jax.ShapeDtypeStruct((m, d), h_dtype)],
        scratch_shapes=[pltpu.VMEM((tm, d), F32)],
        compiler_params=_params("parallel", "arbitrary"),
        name="ffn_residual",
    )(h, w1, w3, w2, x, gate, *_next_norm_args(norm, d))


def _router_kernel(h_ref, w_ref, b_ref, sel_ref, wts_ref):
    logits = jnp.dot(h_ref[...].astype(BF16), w_ref[...], preferred_element_type=F32) + b_ref[...]
    ne = logits.shape[-1]
    idx = lax.broadcasted_iota(jnp.int32, logits.shape, 1)
    v1 = jnp.max(logits, axis=-1, keepdims=True)
    i1 = jnp.min(jnp.where(logits == v1, idx, ne), axis=-1, keepdims=True)
    rest = jnp.where(idx == i1, -jnp.inf, logits)
    v2 = jnp.max(rest, axis=-1, keepdims=True)
    i2 = jnp.min(jnp.where(rest == v2, idx, ne), axis=-1, keepdims=True)
    e2 = jnp.exp(v2 - v1)
    sel_ref[...] = jnp.concatenate([i1, i2], axis=1)
    wts_ref[...] = jnp.concatenate([1.0 / (1.0 + e2), e2 / (1.0 + e2)], axis=1)


def router(h, w_r, b_r):
    m, d = h.shape
    ne = w_r.shape[1]
    tm = _tile(m, ROW_TILE)
    return pl.pallas_call(
        _router_kernel,
        grid=(m // tm,),
        in_specs=[
            pl.BlockSpec((tm, d), lambda i: (i, 0)),
            pl.BlockSpec((d, ne), lambda i: (0, 0)),
            pl.BlockSpec((1, ne), lambda i: (0, 0)),
        ],
        out_specs=[pl.BlockSpec((tm, TOP_K), lambda i: (i, 0)), pl.BlockSpec((tm, TOP_K), lambda i: (i, 0))],
        out_shape=[jax.ShapeDtypeStruct((m, TOP_K), jnp.int32), jax.ShapeDtypeStruct((m, TOP_K), F32)],
        compiler_params=_params("parallel"),
        name="router",
    )(h, w_r.astype(BF16), b_r.reshape(1, ne))


def dispatch_plan(sel, tm):
    m = sel.shape[0]
    ne = N_EXPERTS
    npairs = m * TOP_K
    rows = npairs + ne * tm
    e_flat = sel.reshape(npairs)
    onehot = (e_flat[:, None] == jnp.arange(ne, dtype=jnp.int32)[None, :]).astype(jnp.int32)
    rank = jnp.cumsum(onehot, axis=0) - onehot
    counts = jnp.sum(onehot, axis=0)
    padded = (counts + tm - 1) // tm * tm
    ends = jnp.cumsum(padded)
    pos = (ends - padded)[e_flat] + jnp.sum(rank * onehot, axis=1)
    tile_start = jnp.arange(rows // tm, dtype=jnp.int32) * tm
    tile_expert = jnp.minimum(jnp.searchsorted(ends, tile_start, side="right"), ne - 1).astype(jnp.int32)
    meta = jnp.concatenate([tile_expert, (ends[-1:] // tm).astype(jnp.int32)])
    return pos.reshape(m, TOP_K), meta, rows


def _wait_rows(src_ref, dst_ref, sem, n):
    pltpu.make_async_copy(src_ref.at[pl.ds(0, n)], dst_ref.at[pl.ds(0, n)], sem).wait()


def _scatter_rows_kernel(pos_ref, h_ref, init_ref, o_ref, sem):
    del init_ref
    tm = h_ref.shape[0]

    def issue(r, carry):
        for k in range(TOP_K):
            pltpu.make_async_copy(h_ref.at[pl.ds(r, 1)], o_ref.at[pl.ds(pos_ref[0, k, r], 1)], sem).start()
        return carry

    lax.fori_loop(0, tm, issue, 0, unroll=8)
    for k in range(TOP_K):
        _wait_rows(h_ref, o_ref, sem, tm)


def scatter_rows(h, pos, rows, tm):
    m, d = h.shape
    pos_t = pos.reshape(m // tm, tm, TOP_K).transpose(0, 2, 1)
    return pl.pallas_call(
        _scatter_rows_kernel,
        grid=(m // tm,),
        in_specs=[
            pl.BlockSpec((1, TOP_K, tm), lambda i: (i, 0, 0), memory_space=pltpu.SMEM),
            pl.BlockSpec((tm, d), lambda i: (i, 0)),
            pl.BlockSpec(memory_space=pl.ANY),
        ],
        out_specs=pl.BlockSpec(memory_space=pl.ANY),
        out_shape=jax.ShapeDtypeStruct((rows, d), h.dtype),
        scratch_shapes=[pltpu.SemaphoreType.DMA(())],
        input_output_aliases={2: 0},
        compiler_params=_params("arbitrary"),
        name="scatter_rows",
    )(pos_t, h, jnp.zeros((rows, d), h.dtype))


def _expert_ffn_kernel(meta_ref, h_ref, w1_ref, w3_ref, w2_ref, o_ref, hb_ref):
    i, f = pl.program_id(0), pl.program_id(1)
    used = i < meta_ref[meta_ref.shape[0] - 1]

    @pl.when(f == 0)
    def _():
        hb_ref[...] = h_ref[...].astype(BF16)
        o_ref[...] = jnp.zeros(o_ref.shape, o_ref.dtype)

    @pl.when(used)
    def _():
        o_ref[...] += _swiglu_out(hb_ref, w1_ref, w3_ref, w2_ref)


def expert_ffn(hs, w1, w3, w2, meta, tm):
    r, d = hs.shape
    fdim = w1.shape[2]
    tf = _tile(fdim, EXPERT_COL_TILE)
    grid_spec = pltpu.PrefetchScalarGridSpec(
        num_scalar_prefetch=1,
        grid=(r // tm, fdim // tf),
        in_specs=[
            pl.BlockSpec((tm, d), lambda i, f, meta: (i, 0)),
            pl.BlockSpec((1, d, tf), lambda i, f, meta: (meta[i], 0, f)),
            pl.BlockSpec((1, d, tf), lambda i, f, meta: (meta[i], 0, f)),
            pl.BlockSpec((1, tf, d), lambda i, f, meta: (meta[i], f, 0)),
        ],
        out_specs=pl.BlockSpec((tm, d), lambda i, f, meta: (i, 0)),
        scratch_shapes=[pltpu.VMEM((tm, d), BF16)],
    )
    return pl.pallas_call(
        _expert_ffn_kernel,
        grid_spec=grid_spec,
        out_shape=jax.ShapeDtypeStruct((r, d), F32),
        compiler_params=_params("parallel", "arbitrary"),
        name="expert_ffn",
    )(meta, hs, w1, w3, w2)


def _combine_kernel(pos_ref, ys_ref, w_ref, x_ref, g_ref, ng_ref, nsh_ref, nsc_ref, o_ref, hn_ref, buf_ref, sem):
    tm = x_ref.shape[0]

    def issue(r, carry):
        for k in range(TOP_K):
            pltpu.make_async_copy(ys_ref.at[pl.ds(pos_ref[0, k, r], 1)], buf_ref.at[pl.ds(k * tm + r, 1)], sem).start()
        return carry

    lax.fori_loop(0, tm, issue, 0, unroll=8)
    _wait_rows(ys_ref, buf_ref, sem, TOP_K * tm)
    w = w_ref[...]
    y = w[:, 0:1] * buf_ref[pl.ds(0, tm), :]
    for k in range(1, TOP_K):
        y = y + w[:, k:k + 1] * buf_ref[pl.ds(k * tm, tm), :]
    _store_residual(x_ref[...] + g_ref[0] * y, o_ref, hn_ref, ng_ref, nsh_ref, nsc_ref)


def combine_residual(ys, pos, wts, x, gate, rows_per_batch, norm, h_dtype):
    m, d = x.shape
    tm = _tile(rows_per_batch, COMBINE_ROW_TILE)
    per = rows_per_batch // tm
    pos_t = pos.reshape(m // tm, tm, TOP_K).transpose(0, 2, 1)
    row = pl.BlockSpec((tm, d), lambda i: (i, 0))
    per_batch = pl.BlockSpec((1, 1, d), lambda i: (i // per, 0, 0))
    return pl.pallas_call(
        _combine_kernel,
        grid=(m // tm,),
        in_specs=[
            pl.BlockSpec((1, TOP_K, tm), lambda i: (i, 0, 0), memory_space=pltpu.SMEM),
            pl.BlockSpec(memory_space=pl.ANY),
            pl.BlockSpec((tm, TOP_K), lambda i: (i, 0)),
            row, per_batch,
            pl.BlockSpec((1, d), lambda i: (0, 0)), per_batch, per_batch,
        ],
        out_specs=[row, row],
        out_shape=[jax.ShapeDtypeStruct((m, d), F32), jax.ShapeDtypeStruct((m, d), h_dtype)],
        scratch_shapes=[pltpu.VMEM((TOP_K * tm, d), F32), pltpu.SemaphoreType.DMA(())],
        compiler_params=_params("arbitrary"),
        name="combine_residual",
    )(pos_t, ys, wts, x, gate, *_next_norm_args(norm, d))


def moe_residual(h, w_r, b_r, w1, w3, w2, x, gate, rows_per_batch, norm, h_dtype):
    m = h.shape[0]
    tm = EXPERT_ROW_TILE if m * TOP_K >= N_EXPERTS * EXPERT_ROW_TILE else SMALL_EXPERT_ROW_TILE
    sel, wts = router(h, w_r, b_r)
    pos, meta, rows = dispatch_plan(sel, tm)
    hs = scatter_rows(h, pos, rows, _tile(rows_per_batch, ROW_TILE))
    ys = expert_ffn(hs, w1, w3, w2, meta, tm)
    return combine_residual(ys, pos, wts, x, gate, rows_per_batch, norm, h_dtype)


def _pad_keys(parts, tk):
    n = sum(t.shape[1] for t in parts)
    pad = -n % tk
    if pad:
        parts = parts + [jnp.zeros((parts[0].shape[0], pad, parts[0].shape[2]), parts[0].dtype)]
    return jnp.concatenate(parts, axis=1) if len(parts) > 1 else parts[0]


def _mixer(x, n, h, proj, keys, values, lam, lam_init, tq, tk, q, p):
    m, d = x.shape
    n_keys = sum(t.shape[1] for t in keys)
    att = diff_attention(q, keys_transposed(_pad_keys(keys, tk), tk), _pad_keys(values, tk), n_keys, lam,
                         p["subln_g"], 1 - lam_init, tq)
    sgu = spatial_gating(proj, p["sgu_ln_g"], p["sgu_ln_b"], p["w_spatial"], p["b_spatial"])
    conv = conformer_conv(proj, p["conv_w"], p["conv_b"], p["conv_ln_g"], p["conv_ln_b"])
    y = merge_branches(h, att.reshape(m, -1), sgu.reshape(m, -1), conv.reshape(m, -1),
                       p["w_gate"], p["b_gate"], p["w_att_out"], p["w_sgu_out"], p["w_conv_out"])
    norm2 = (p["norm2_g"], p["sh2"], p["sc2"])
    return proj_residual(y, p["w_o"], x, p["g1"], n, norm2, F32 if p["moe"] else BF16)


def _channel(x, h, n, p, next_norm, next_dtype):
    if p["moe"]:
        return moe_residual(h, p["router_w"], p["router_b"], p["w1"], p["w3"], p["w2"], x, p["g2"], n,
                            next_norm, next_dtype)
    return ffn_residual(h, p["w1"], p["w3"], p["w2"], x, p["g2"], n, next_norm, next_dtype)


def kernel(x, c, ctx, c_ctx, w_mod, b_mod, norm1_g, norm2_g, w_in, lam_q1, lam_k1, lam_q2, lam_k2, subln_g, w_att_out, sgu_ln_g, sgu_ln_b, w_spatial, b_spatial, w_sgu_out, conv_w, conv_b, conv_ln_g, conv_ln_b, w_conv_out, w_gate, b_gate, w_o, ffn_w1, ffn_w3, ffn_w2, router_w, router_b, moe_w1, moe_w3, moe_w2, final_g):
    b, s, d = x.shape
    nc = ctx.shape[1]
    depth = w_in.shape[0]
    cos, sin = rope_tables(s)
    ones, zeros = jnp.ones((nc, LANES), F32), jnp.zeros((nc, LANES), F32)

    c_rows = jnp.concatenate([c, c_ctx[None, :], jnp.zeros((SUBLANES - b - 1, d), F32)], axis=0)
    mods = modulation(c_rows, w_mod, b_mod)
    lats = [[t[:, None, :] for t in jnp.split(mods[i, :b], 6, axis=-1)] for i in range(depth)]
    cxs = [[jnp.broadcast_to(t[:, None, :], (b, 1, d)) for t in jnp.split(mods[i, b:b + 1], 6, axis=-1)]
           for i in range(depth)]
    zero = jnp.zeros((b, 1, d), F32)

    x = x.reshape(b * s, d)
    xc = ctx.reshape(b * nc, d)
    h_lat = normmod(x.reshape(b, s, d), norm1_g[0], lats[0][0], lats[0][1], BF16).reshape(b * s, d)
    h_ctx = normmod(ctx, norm1_g[0], cxs[0][0], cxs[0][1], BF16).reshape(b * nc, d)
    for i in range(depth):
        last = i == depth - 1
        lat, cx = lats[i], cxs[i]
        lam_init = 0.8 - 0.6 * math.exp(-0.3 * i)
        lam = (jnp.exp(jnp.sum(lam_q1[i] * lam_k1[i]).astype(F32))
               - jnp.exp(jnp.sum(lam_q2[i] * lam_k2[i]).astype(F32)) + lam_init)
        shared = dict(
            norm2_g=norm2_g[i], subln_g=subln_g[i], sgu_ln_g=sgu_ln_g[i], sgu_ln_b=sgu_ln_b[i],
            w_spatial=w_spatial[i], b_spatial=b_spatial[i], conv_w=conv_w[i], conv_b=conv_b[i],
            conv_ln_g=conv_ln_g[i], conv_ln_b=conv_ln_b[i],
            w_gate=w_gate[i].astype(BF16), b_gate=b_gate[i], w_att_out=w_att_out[i].astype(BF16),
            w_sgu_out=w_sgu_out[i].astype(BF16), w_conv_out=w_conv_out[i].astype(BF16), w_o=w_o[i].astype(BF16),
            moe=i % 2 == 1)
        j = i // 2
        if i % 2 == 0:
            shared.update(w1=ffn_w1[j][None].astype(BF16), w3=ffn_w3[j][None].astype(BF16),
                          w2=ffn_w2[j][None].astype(BF16))
        else:
            shared.update(w1=moe_w1[j].astype(BF16), w3=moe_w3[j].astype(BF16), w2=moe_w2[j].astype(BF16),
                          router_w=router_w[j], router_b=router_b[j])
        p_lat = dict(shared, g1=lat[2], sh2=lat[3], sc2=lat[4], g2=lat[5])
        p_ctx = dict(shared, g1=cx[2], sh2=cx[3], sc2=cx[4], g2=cx[5])
        w_in_i = w_in[i].astype(BF16)

        proj_l = matmul(h_lat, w_in_i, BF16).reshape(b, s, IN_COLS)
        proj_c = matmul(h_ctx, w_in_i, BF16).reshape(b, nc, IN_COLS)
        rope_l = rope_qk(proj_l, cos, sin)
        rope_c = rope_qk(proj_c, ones, zeros)
        v_l = proj_l[..., 2 * ATT_WIDTH:3 * ATT_WIDTH]
        v_c = proj_c[..., 2 * ATT_WIDTH:3 * ATT_WIDTH]
        x, h2_lat = _mixer(x, s, h_lat, proj_l, [rope_l[1], rope_c[1]], [v_l, v_c], lam, lam_init,
                           _tile(s, ATT_QUERY_TILE), ATT_KEY_TILE, rope_l[0], p_lat)
        if not last:
            xc, h2_ctx = _mixer(xc, nc, h_ctx, proj_c, [rope_c[1]], [v_c], lam, lam_init, nc, nc, rope_c[0], p_ctx)

        if last:
            _, out = _channel(x, h2_lat, s, p_lat, (final_g, zero, zero), F32)
            return out.reshape(b, s, d)
        x, h_lat = _channel(x, h2_lat, s, p_lat, (norm1_g[i + 1], lats[i + 1][0], lats[i + 1][1]), BF16)
        xc, h_ctx = _channel(xc, h2_ctx, nc, p_ctx, (norm1_g[i + 1], cxs[i + 1][0], cxs[i + 1][1]), BF16)
```
